```python
import math
import jax, jax.numpy as jnp
from jax import lax
import numpy as np

D_MODEL = 2048
BATCH = 2
SEQ = 16384
DEPTH = 1
DEC_BATCH = 16
DEC_SEQ = 16
PAST_LEN = 2048

CHUNK = 64
Q_BLOCK = 128
POOL_WINDOWS = (2, 4, 8, 16)
POOL_GROUPS = 4
POOL_GROUP_DIM = D_MODEL // 16
POOL_WIDTH = POOL_GROUPS * POOL_GROUP_DIM
POOL_HIST = max(POOL_WINDOWS) - 1
ATTN_WIDTH = D_MODEL - POOL_WIDTH
N_HEADS = 12
V_HEAD = ATTN_WIDTH // N_HEADS
D_HEAD = V_HEAD // 2
MIX_WIDTH = POOL_WIDTH + ATTN_WIDTH
IN_WIDTH = 2 * POOL_WIDTH + 4 * ATTN_WIDTH
NORM_EPS = 1e-6
SUBLN_EPS = 1e-5
MASK_VALUE = -1e30

kernel_name = "hybrid_pool_diffattn_stream_step"


def alibi_slopes(n):
    def pow2_slopes(m):
        start = 2.0 ** (-8.0 / m)
        return [start ** (i + 1) for i in range(m)]
    if math.log2(n).is_integer():
        s = pow2_slopes(n)
    else:
        c = 2 ** int(math.floor(math.log2(n)))
        s = pow2_slopes(c) + pow2_slopes(2 * c)[0::2][: n - c]
    return jnp.asarray(np.array(s, dtype=np.float32))


def rmsnorm(x, w, eps=NORM_EPS):
    x32 = x.astype(jnp.float32)
    y = x32 * lax.rsqrt(jnp.mean(x32 * x32, axis=-1, keepdims=True) + eps) * w.astype(jnp.float32)
    return y.astype(x.dtype)


def split_in(h, w_in):
    b, t, _ = h.shape
    z = jnp.einsum('btd,de->bte', h, w_in)
    o = np.cumsum([0, POOL_WIDTH, POOL_WIDTH, ATTN_WIDTH, ATTN_WIDTH, ATTN_WIDTH, ATTN_WIDTH])
    u = z[..., o[0]:o[1]]
    g_pool = z[..., o[1]:o[2]]
    q = z[..., o[2]:o[3]].reshape(b, t, N_HEADS, 2, D_HEAD)
    k = z[..., o[3]:o[4]].reshape(b, t, N_HEADS, 2, D_HEAD)
    v = z[..., o[4]:o[5]].reshape(b, t, N_HEADS, V_HEAD)
    g_attn = z[..., o[5]:o[6]]
    return u, g_pool, q, k, v, g_attn


def pool_mixer(u, hist, hist_valid, w_pool, pool_scale):
    b, t, _ = u.shape
    ext = jnp.concatenate([hist, u], axis=1)
    ext32 = ext.astype(jnp.float32)
    cs = jnp.concatenate([jnp.zeros_like(ext32[:, :1]), lax.cumsum(ext32, axis=1)], axis=1)
    end = cs[:, POOL_HIST + 1:]
    j = jnp.arange(t)
    u32 = u.astype(jnp.float32)
    diffs = []
    for g, w in enumerate(POOL_WINDOWS):
        sl = slice(g * POOL_GROUP_DIM, (g + 1) * POOL_GROUP_DIM)
        start = cs[:, POOL_HIST + 1 - w:POOL_HIST + 1 - w + t, sl]
        count = jnp.minimum(w, hist_valid + j + 1).astype(jnp.float32)
        mean = (end[..., sl] - start) / count[None, :, None]
        diffs.append(mean - u32[..., sl])
    d = jnp.stack(diffs, axis=2)
    y = jnp.einsum('btgc,gcd->btgd', d, w_pool.astype(jnp.float32)).reshape(b, t, POOL_WIDTH)
    y = y * pool_scale.astype(jnp.float32)
    return y.astype(u.dtype), ext[:, -POOL_HIST:]


def diff_attend(q, k, v, qpos, kpos, slopes, lam, lam_init, subln_w):
    s = jnp.einsum('bqhcd,bkhcd->bhcqk', q, k, preferred_element_type=jnp.float32) * (D_HEAD ** -0.5)
    rel = jnp.abs(qpos[:, None] - kpos[None, :]).astype(jnp.float32)
    bias = -slopes[:, None, None] * rel[None]
    allowed = (kpos[None, :] // CHUNK) <= (qpos[:, None] // CHUNK)
    s = jnp.where(allowed[None, None, None], s + bias[None, :, None], MASK_VALUE)
    p = jax.nn.softmax(s, axis=-1)
    pd = p[:, :, 0] - lam * p[:, :, 1]
    o = jnp.einsum('bhqk,bkhe->bqhe', pd.astype(v.dtype), v, preferred_element_type=jnp.float32)
    o = o * lax.rsqrt(jnp.mean(o * o, axis=-1, keepdims=True) + SUBLN_EPS) * subln_w.astype(jnp.float32)
    return o * (1.0 - lam_init)


def merge(pool_out, g_pool, attn_out, g_attn, w_out):
    mix = jnp.concatenate([jax.nn.silu(g_pool) * pool_out, jax.nn.silu(g_attn) * attn_out], axis=-1)
    return jnp.einsum('btm,md->btd', mix, w_out)


def setup_inputs(seed: int = 0) -> dict:
    key = jax.random.key(seed)
    ks = jax.random.split(key, 16)
    f32 = jnp.float32
    return {
        "x_prompt": jax.random.normal(ks[0], (BATCH, SEQ, D_MODEL), f32),
        "x_sample": jax.random.normal(ks[1], (DEC_BATCH, DEC_SEQ, D_MODEL), f32),
        "cache_k": jax.random.normal(ks[2], (DEPTH, DEC_BATCH, PAST_LEN, N_HEADS, 2 * D_HEAD), f32),
        "cache_v": jax.random.normal(ks[3], (DEPTH, DEC_BATCH, PAST_LEN, N_HEADS, V_HEAD), f32),
        "state_pool": jax.random.normal(ks[4], (DEPTH, DEC_BATCH, POOL_HIST, POOL_WIDTH), f32),
        "norm_w": 1.0 + 0.1 * jax.random.normal(ks[5], (DEPTH, D_MODEL), f32),
        "w_in": jax.random.normal(ks[6], (DEPTH, D_MODEL, IN_WIDTH), f32) * D_MODEL ** -0.5,
        "w_pool": jax.random.normal(ks[7], (DEPTH, POOL_GROUPS, POOL_GROUP_DIM, POOL_GROUP_DIM), f32) * POOL_GROUP_DIM ** -0.5,
        "pool_scale": 1.0 + 0.1 * jax.random.normal(ks[8], (DEPTH, POOL_WIDTH), f32),
        "lambda_q1": 0.1 * jax.random.normal(ks[9], (DEPTH, D_HEAD), f32),
        "lambda_k1": 0.1 * jax.random.normal(ks[10], (DEPTH, D_HEAD), f32),
        "lambda_q2": 0.1 * jax.random.normal(ks[11], (DEPTH, D_HEAD), f32),
        "lambda_k2": 0.1 * jax.random.normal(ks[12], (DEPTH, D_HEAD), f32),
        "subln_w": 1.0 + 0.1 * jax.random.normal(ks[13], (DEPTH, V_HEAD), f32),
        "w_out": jax.random.normal(ks[14], (DEPTH, MIX_WIDTH, D_MODEL), f32) * MIX_WIDTH ** -0.5,
        "final_norm_w": 1.0 + 0.1 * jax.random.normal(ks[15], (D_MODEL,), f32),
    }


def reference(x_prompt, x_sample, cache_k, cache_v, state_pool, norm_w, w_in, w_pool, pool_scale,
              lambda_q1, lambda_k1, lambda_q2, lambda_k2, subln_w, w_out, final_norm_w):
    f32 = jnp.float32
    slopes = alibi_slopes(N_HEADS)
    b, s, _ = x_prompt.shape
    db, t, _ = x_sample.shape
    past = cache_k.shape[2]
    nb = s // Q_BLOCK
    hp, hs = x_prompt, x_sample
    kp_l, vp_l, pp_l, ks_l, vs_l, ps_l = [], [], [], [], [], []
    for l in range(DEPTH):
        lam_init = 0.8 - 0.6 * math.exp(-0.3 * l)
        lam = (jnp.exp(jnp.sum(lambda_q1[l].astype(f32) * lambda_k1[l].astype(f32)))
               - jnp.exp(jnp.sum(lambda_q2[l].astype(f32) * lambda_k2[l].astype(f32))) + lam_init)

        u, gpool, q, k, v, gattn = split_in(rmsnorm(hp, norm_w[l]), w_in[l])
        pool_out, pool_state_p = pool_mixer(u, jnp.zeros((b, POOL_HIST, POOL_WIDTH), u.dtype), 0,
                                            w_pool[l], pool_scale[l])
        kpos = jnp.arange(s)
        qblocks = q.reshape(b, nb, Q_BLOCK, N_HEADS, 2, D_HEAD).swapaxes(0, 1)

        def block_fn(args, k=k, v=v, lam=lam, lam_init=lam_init, sw=subln_w[l]):
            qb, i = args
            qpos = i * Q_BLOCK + jnp.arange(Q_BLOCK)
            return diff_attend(qb, k, v, qpos, kpos, slopes, lam, lam_init, sw).astype(v.dtype)

        ob = lax.map(block_fn, (qblocks, jnp.arange(nb)))
        attn_out = ob.swapaxes(0, 1).reshape(b, s, ATTN_WIDTH)
        hp = hp + merge(pool_out, gpool, attn_out, gattn, w_out[l])
        kp_l.append(k.reshape(b, s, N_HEADS, 2 * D_HEAD))
        vp_l.append(v)
        pp_l.append(pool_state_p)

        u2, gpool2, q2, k2, v2, gattn2 = split_in(rmsnorm(hs, norm_w[l]), w_in[l])
        pool_out2, pool_state_s = pool_mixer(u2, state_pool[l].astype(u2.dtype), POOL_HIST,
                                             w_pool[l], pool_scale[l])
        k_all = jnp.concatenate([cache_k[l].reshape(db, past, N_HEADS, 2, D_HEAD).astype(k2.dtype), k2], axis=1)
        v_all = jnp.concatenate([cache_v[l].astype(v2.dtype), v2], axis=1)
        qpos2 = past + jnp.arange(t)
        kpos2 = jnp.arange(past + t)
        attn2 = diff_attend(q2, k_all, v_all, qpos2, kpos2, slopes, lam, lam_init, subln_w[l])
        attn2 = attn2.astype(v2.dtype).reshape(db, t, ATTN_WIDTH)
        hs = hs + merge(pool_out2, gpool2, attn2, gattn2, w_out[l])
        ks_l.append(k2.reshape(db, t, N_HEADS, 2 * D_HEAD))
        vs_l.append(v2)
        ps_l.append(pool_state_s)

    y_prompt = rmsnorm(hp, final_norm_w)
    y_sample = rmsnorm(hs, final_norm_w)
    k_prompt = jnp.stack(kp_l)
    v_prompt = jnp.stack(vp_l)
    pool_prompt = jnp.stack(pp_l)
    k_sample = jnp.stack(ks_l)
    v_sample = jnp.stack(vs_l)
    pool_sample = jnp.stack(ps_l)
    return (y_prompt, y_sample, k_prompt, v_prompt, pool_prompt, k_sample, v_sample, pool_sample)
```

```python
import functools
import math

import jax
import jax.numpy as jnp
import numpy as np
from jax import lax
from jax.experimental import pallas as pl
from jax.experimental.pallas import tpu as pltpu

F32 = jnp.float32
BF16 = jnp.bfloat16

D_MODEL = 2048
CHUNK = 64
CHUNK_SHIFT = 6
POOL_WINDOWS = (2, 4, 8, 16)
POOL_GROUP_DIM = 128
POOL_WIDTH = len(POOL_WINDOWS) * POOL_GROUP_DIM
POOL_HIST = max(POOL_WINDOWS) - 1
HALO = POOL_HIST + 1
ATTN_WIDTH = D_MODEL - POOL_WIDTH
N_HEADS = 12
V_HEAD = ATTN_WIDTH // N_HEADS
D_HEAD = V_HEAD // 2
IN_WIDTH = 2 * POOL_WIDTH + 4 * ATTN_WIDTH
NORM_EPS = 1e-6
SUBLN_EPS = 1e-5
MASK_VALUE = -1e30
LOG2E = math.log2(math.e)
Q_SCALE = (D_HEAD ** -0.5) * LOG2E

COL_TILE = 512
N_COL_TILES = IN_WIDTH // COL_TILE
ATTN_COL_TILES = ATTN_WIDTH // COL_TILE
SEG_GP, SEG_Q, SEG_K, SEG_V, SEG_GA = 1, 2, 2 + ATTN_COL_TILES, 2 + 2 * ATTN_COL_TILES, 2 + 3 * ATTN_COL_TILES

MIB = 1024 * 1024


def _alibi_sigmas():
    def pow2_slopes(m):
        start = 2.0 ** (-8.0 / m)
        return [start ** (i + 1) for i in range(m)]

    n = N_HEADS
    if math.log2(n).is_integer():
        s = pow2_slopes(n)
    else:
        c = 2 ** int(math.floor(math.log2(n)))
        s = pow2_slopes(c) + pow2_slopes(2 * c)[0::2][: n - c]
    return jnp.asarray(np.array(s, dtype=np.float32) * np.float32(LOG2E))


def _lam_init(layer):
    return 0.8 - 0.6 * math.exp(-0.3 * layer)


def _lam(lq1_ref, lk1_ref, lq2_ref, lk2_ref, lam_init):
    a = jnp.sum(lq1_ref[...] * lk1_ref[...], axis=-1, keepdims=True)
    b = jnp.sum(lq2_ref[...] * lk2_ref[...], axis=-1, keepdims=True)
    return jnp.exp(a) - jnp.exp(b) + lam_init


def _proj_kernel(x_ref, nw_ref, w_ref, u_ref, gp_ref, q_ref, k_ref, v_ref, ga_ref, *rest, attn_layouts):
    if attn_layouts:
        kb_ref, vt_ref, h_scr = rest
    else:
        (h_scr,) = rest
    j = pl.program_id(2)

    @pl.when(j == 0)
    def _():
        x = x_ref[0]
        ms = jnp.mean(x * x, axis=-1, keepdims=True)
        h_scr[...] = (x * lax.rsqrt(ms + NORM_EPS) * nw_ref[...]).astype(BF16)

    z = jnp.dot(h_scr[...], w_ref[...], preferred_element_type=F32)

    @pl.when(j == 0)
    def _():
        u_ref[0] = z

    @pl.when(j == SEG_GP)
    def _():
        gp_ref[0] = z

    @pl.when((j >= SEG_Q) & (j < SEG_K))
    def _():
        q_ref[0] = (z * Q_SCALE).astype(BF16)

    @pl.when((j >= SEG_K) & (j < SEG_V))
    def _():
        k_ref[0] = z
        if attn_layouts:
            kb_ref[0] = z.astype(BF16)

    @pl.when((j >= SEG_V) & (j < SEG_GA))
    def _():
        v_ref[0] = z
        if attn_layouts:
            vt_ref[0] = z.T.astype(BF16)

    @pl.when(j >= SEG_GA)
    def _():
        ga_ref[0] = z


def _project(x, norm_w, w_in_bf16, *, tm, attn_layouts):
    b, s, d = x.shape
    assert s % tm == 0 and d == D_MODEL
    grid = (b, s // tm, N_COL_TILES)

    def seg_map(first):
        return lambda bi, i, j: (bi, i, jnp.clip(j - first, 0, ATTN_COL_TILES - 1))

    row_tile = lambda bi, i, j: (bi, i, 0)
    out_shape = [
        jax.ShapeDtypeStruct((b, s, POOL_WIDTH), F32),
        jax.ShapeDtypeStruct((b, s, POOL_WIDTH), F32),
        jax.ShapeDtypeStruct((b, s, ATTN_WIDTH), BF16),
        jax.ShapeDtypeStruct((b, s, ATTN_WIDTH), F32),
        jax.ShapeDtypeStruct((b, s, ATTN_WIDTH), F32),
        jax.ShapeDtypeStruct((b, s, ATTN_WIDTH), F32),
    ]
    out_specs = [
        pl.BlockSpec((1, tm, COL_TILE), row_tile),
        pl.BlockSpec((1, tm, COL_TILE), row_tile),
        pl.BlockSpec((1, tm, COL_TILE), seg_map(SEG_Q)),
        pl.BlockSpec((1, tm, COL_TILE), seg_map(SEG_K)),
        pl.BlockSpec((1, tm, COL_TILE), seg_map(SEG_V)),
        pl.BlockSpec((1, tm, COL_TILE), seg_map(SEG_GA)),
    ]
    if attn_layouts:
        out_shape += [
            jax.ShapeDtypeStruct((b, s, ATTN_WIDTH), BF16),
            jax.ShapeDtypeStruct((b, ATTN_WIDTH, s), BF16),
        ]
        out_specs += [
            pl.BlockSpec((1, tm, COL_TILE), seg_map(SEG_K)),
            pl.BlockSpec((1, COL_TILE, tm), lambda bi, i, j: (bi, jnp.clip(j - SEG_V, 0, ATTN_COL_TILES - 1), i)),
        ]
    vmem = 2 * tm * d * 4 + tm * d * 2 + 2 * d * COL_TILE * 2 + 2 * tm * COL_TILE * (5 * 4 + 3 * 2) + 4 * tm * COL_TILE * 4
    return pl.pallas_call(
        functools.partial(_proj_kernel, attn_layouts=attn_layouts),
        grid=grid,
        in_specs=[
            pl.BlockSpec((1, tm, d), row_tile),
            pl.BlockSpec((1, d), lambda bi, i, j: (0, 0)),
            pl.BlockSpec((d, COL_TILE), lambda bi, i, j: (0, j)),
        ],
        out_specs=out_specs,
        out_shape=out_shape,
        scratch_shapes=[pltpu.VMEM((tm, d), BF16)],
        compiler_params=pltpu.CompilerParams(
            dimension_semantics=("arbitrary", "arbitrary", "arbitrary"),
            vmem_limit_bytes=vmem + 8 * MIB,
        ),
        name="proj",
    )(x, norm_w.reshape(1, d), w_in_bf16)


def _pool_kernel(u_ref, prev_ref, hist_ref, wp_ref, ps_ref, o_ref, ext_scr, *, tm, hist_valid):
    i = pl.program_id(1)
    ext_scr[0:HALO, :] = jnp.where(i == 0, hist_ref[0], prev_ref[0])
    ext_scr[HALO:HALO + tm, :] = u_ref[0]
    frame = i * tm + lax.broadcasted_iota(jnp.int32, (tm, POOL_GROUP_DIM), 0)
    for g, w in enumerate(POOL_WINDOWS):
        lanes = slice(g * POOL_GROUP_DIM, (g + 1) * POOL_GROUP_DIM)
        cur = ext_scr[HALO:HALO + tm, lanes]
        acc = cur
        for dlt in range(1, w):
            acc = acc + ext_scr[HALO - dlt:HALO - dlt + tm, lanes]
        count = jnp.minimum(w, hist_valid + frame + 1).astype(F32)
        diff = acc / count - cur
        y = jnp.dot(diff.astype(BF16), wp_ref[g], preferred_element_type=F32)
        o_ref[0, :, lanes] = y * ps_ref[:, lanes]


def _pool_mix(u, hist, w_pool_bf16, pool_scale, *, tm, hist_valid):
    b, t, c = u.shape
    assert t % tm == 0 and tm % HALO == 0
    blocks_per_tile = tm // HALO
    return pl.pallas_call(
        functools.partial(_pool_kernel, tm=tm, hist_valid=hist_valid),
        grid=(b, t // tm),
        in_specs=[
            pl.BlockSpec((1, tm, c), lambda bi, i: (bi, i, 0)),
            pl.BlockSpec((1, HALO, c), lambda bi, i: (bi, jnp.maximum(i * blocks_per_tile - 1, 0), 0)),
            pl.BlockSpec((1, HALO, c), lambda bi, i: (bi, 0, 0)),
            pl.BlockSpec((len(POOL_WINDOWS), POOL_GROUP_DIM, POOL_GROUP_DIM), lambda bi, i: (0, 0, 0)),
            pl.BlockSpec((1, c), lambda bi, i: (0, 0)),
        ],
        out_specs=pl.BlockSpec((1, tm, c), lambda bi, i: (bi, i, 0)),
        out_shape=jax.ShapeDtypeStruct((b, t, c), F32),
        scratch_shapes=[pltpu.VMEM((HALO + tm, c), F32)],
        compiler_params=pltpu.CompilerParams(dimension_semantics=("arbitrary", "arbitrary")),
        name="pool",
    )(u, u, hist, w_pool_bf16, pool_scale.reshape(1, c))


def _attn_kernel(sig_ref, q_ref, k_ref, vt_ref, lq1_ref, lk1_ref, lq2_ref, lk2_ref, sw_ref, o_ref,
                 qp_scr, bias_scr, m_scr, l_scr, acc_scr, *, tq, lam_init):
    h = pl.program_id(1)
    qi = pl.program_id(2)
    sigma = sig_ref[h]
    decay = sigma * tq

    @pl.when(qi == 0)
    def _():
        kj = lax.broadcasted_iota(jnp.int32, (tq, tq), 0)
        qq = lax.broadcasted_iota(jnp.int32, (tq, tq), 1)
        rel = (qq - kj).astype(F32)
        bias_scr[0] = -sigma * rel
        allowed = lax.shift_right_logical(kj, CHUNK_SHIFT) <= lax.shift_right_logical(qq, CHUNK_SHIFT)
        bias_scr[1] = jnp.where(allowed, -sigma * jnp.abs(rel), MASK_VALUE)

    qt = q_ref[0].astype(F32).T
    row = lax.broadcasted_iota(jnp.int32, (V_HEAD, tq), 0)
    qp_scr[0] = jnp.where(row < D_HEAD, qt, 0.0).astype(BF16)
    qp_scr[1] = jnp.where(row >= D_HEAD, qt, 0.0).astype(BF16)
    m_scr[...] = jnp.full(m_scr.shape, MASK_VALUE, F32)
    l_scr[...] = jnp.zeros(l_scr.shape, F32)
    acc_scr[...] = jnp.zeros(acc_scr.shape, F32)

    def block(kb, sel):
        k0 = pl.multiple_of(kb * tq, tq)
        kblk = k_ref[0, pl.ds(k0, tq), :]
        vblk = vt_ref[0, :, pl.ds(k0, tq)]
        for c in range(2):
            s = jnp.dot(kblk, qp_scr[c], preferred_element_type=F32)
            sh = s + bias_scr[sel]
            m_old = m_scr[c] - decay
            m_new = jnp.maximum(m_old, jnp.max(sh, axis=0, keepdims=True))
            p = jnp.exp2(sh - m_new)
            alpha = jnp.exp2(m_old - m_new)
            l_scr[c] = alpha * l_scr[c] + jnp.sum(p, axis=0, keepdims=True)
            acc_scr[c] = alpha * acc_scr[c] + jnp.dot(vblk, p.astype(BF16), preferred_element_type=F32)
            m_scr[c] = m_new

    def body(kb, carry):
        block(kb, 0)
        return carry

    lax.fori_loop(0, qi, body, 0)
    block(qi, 1)

    lam = _lam(lq1_ref, lk1_ref, lq2_ref, lk2_ref, lam_init)
    o = acc_scr[0] / l_scr[0] - lam * (acc_scr[1] / l_scr[1])
    o = o * lax.rsqrt(jnp.mean(o * o, axis=0, keepdims=True) + SUBLN_EPS)
    o_ref[0] = o.T * sw_ref[...] * (1.0 - lam_init)


def _prompt_attention(q, kb, vt, lams, subln_w, *, tq, lam_init):
    b, s, _ = q.shape
    assert s % tq == 0 and tq % CHUNK == 0
    lam_spec = pl.BlockSpec((1, D_HEAD), lambda bi, h, i: (0, 0))
    vmem = 2 * 2 * s * V_HEAD * 2 + 2 * tq * tq * 4 + 10 * tq * tq * 4 + 8 * tq * V_HEAD * 4
    return pl.pallas_call(
        functools.partial(_attn_kernel, tq=tq, lam_init=lam_init),
        grid=(b, N_HEADS, s // tq),
        in_specs=[
            pl.BlockSpec(memory_space=pltpu.SMEM),
            pl.BlockSpec((1, tq, V_HEAD), lambda bi, h, i: (bi, i, h)),
            pl.BlockSpec((1, s, V_HEAD), lambda bi, h, i: (bi, 0, h)),
            pl.BlockSpec((1, V_HEAD, s), lambda bi, h, i: (bi, h, 0)),
            lam_spec, lam_spec, lam_spec, lam_spec,
            pl.BlockSpec((1, V_HEAD), lambda bi, h, i: (0, 0)),
        ],
        out_specs=pl.BlockSpec((1, tq, V_HEAD), lambda bi, h, i: (bi, i, h)),
        out_shape=jax.ShapeDtypeStruct((b, s, ATTN_WIDTH), F32),
        scratch_shapes=[
            pltpu.VMEM((2, V_HEAD, tq), BF16),
            pltpu.VMEM((2, tq, tq), F32),
            pltpu.VMEM((2, 1, tq), F32),
            pltpu.VMEM((2, 1, tq), F32),
            pltpu.VMEM((2, V_HEAD, tq), F32),
        ],
        compiler_params=pltpu.CompilerParams(
            dimension_semantics=("arbitrary", "arbitrary", "arbitrary"),
            vmem_limit_bytes=vmem + 8 * MIB,
        ),
        name="attn",
    )(_alibi_sigmas(), q, kb, vt, *lams, subln_w.reshape(1, V_HEAD))


HEADS_PER_STEP = 4


def _decode_kernel(sig_ref, q_ref, kn_ref, vn_ref, kc_ref, vc_ref, lq1_ref, lk1_ref, lq2_ref, lk2_ref, sw_ref,
                   o_ref, *, t, past, lam_init):
    hg = pl.program_id(1)
    lam = _lam(lq1_ref, lk1_ref, lq2_ref, lk2_ref, lam_init)
    qpos_c = past + lax.broadcasted_iota(jnp.int32, (t, past), 0)
    kpos_c = lax.broadcasted_iota(jnp.int32, (t, past), 1)
    qpos_n = past + lax.broadcasted_iota(jnp.int32, (t, t), 0)
    kpos_n = past + lax.broadcasted_iota(jnp.int32, (t, t), 1)

    def bias(qpos, kpos, sigma):
        allowed = lax.shift_right_logical(kpos, CHUNK_SHIFT) <= lax.shift_right_logical(qpos, CHUNK_SHIFT)
        return jnp.where(allowed, -sigma * jnp.abs(qpos - kpos).astype(F32), MASK_VALUE)

    lane = lax.broadcasted_iota(jnp.int32, (t, V_HEAD), 1)
    nt = (((1,), (1,)), ((), ()))
    for hh in range(HEADS_PER_STEP):
        sigma = sig_ref[hg * HEADS_PER_STEP + hh]
        cols = slice(hh * V_HEAD, (hh + 1) * V_HEAD)
        q = q_ref[0, :, cols]
        kc = kc_ref[0, :, cols].astype(BF16)
        vc = vc_ref[0, :, cols].astype(BF16)
        kn = kn_ref[0, :, cols].astype(BF16)
        vn = vn_ref[0, :, cols].astype(BF16)
        bias_c = bias(qpos_c, kpos_c, sigma)
        bias_n = bias(qpos_n, kpos_n, sigma)
        outs = []
        for c in range(2):
            keep = (lane < D_HEAD) if c == 0 else (lane >= D_HEAD)
            qc = jnp.where(keep, q, jnp.zeros_like(q))
            s_c = lax.dot_general(qc, kc, nt, preferred_element_type=F32) + bias_c
            s_n = lax.dot_general(qc, kn, nt, preferred_element_type=F32) + bias_n
            m = jnp.maximum(jnp.max(s_c, axis=-1, keepdims=True), jnp.max(s_n, axis=-1, keepdims=True))
            p_c = jnp.exp2(s_c - m)
            p_n = jnp.exp2(s_n - m)
            l = jnp.sum(p_c, axis=-1, keepdims=True) + jnp.sum(p_n, axis=-1, keepdims=True)
            pv = (jnp.dot(p_c.astype(BF16), vc, preferred_element_type=F32)
                  + jnp.dot(p_n.astype(BF16), vn, preferred_element_type=F32))
            outs.append(pv / l)
        o = outs[0] - lam * outs[1]
        o = o * lax.rsqrt(jnp.mean(o * o, axis=-1, keepdims=True) + SUBLN_EPS)
        o_ref[0, :, cols] = o * sw_ref[...] * (1.0 - lam_init)


def _decode_attention(q, k_new, v_new, cache_k, cache_v, lams, subln_w, *, lam_init):
    db, t, _ = q.shape
    past = cache_k.shape[1]
    width = HEADS_PER_STEP * V_HEAD
    new_spec = pl.BlockSpec((1, t, width), lambda bi, g: (bi, 0, g))
    cache_spec = pl.BlockSpec((1, past, width), lambda bi, g: (bi, 0, g))
    lam_spec = pl.BlockSpec((1, D_HEAD), lambda bi, g: (0, 0))
    vmem = 2 * 2 * past * width * 4 + 8 * past * V_HEAD * 4 + 16 * t * past * 4
    return pl.pallas_call(
        functools.partial(_decode_kernel, t=t, past=past, lam_init=lam_init),
        grid=(db, N_HEADS // HEADS_PER_STEP),
        in_specs=[
            pl.BlockSpec(memory_space=pltpu.SMEM),
            new_spec, new_spec, new_spec, cache_spec, cache_spec,
            lam_spec, lam_spec, lam_spec, lam_spec,
            pl.BlockSpec((1, V_HEAD), lambda bi, g: (0, 0)),
        ],
        out_specs=new_spec,
        out_shape=jax.ShapeDtypeStruct((db, t, ATTN_WIDTH), F32),
        compiler_params=pltpu.CompilerParams(
            dimension_semantics=("arbitrary", "arbitrary"),
            vmem_limit_bytes=vmem + 8 * MIB,
        ),
        name="decode_attn",
    )(_alibi_sigmas(), q, k_new, v_new, cache_k, cache_v, *lams, subln_w.reshape(1, V_HEAD))


def _out_kernel(pool_ref, gp_ref, attn_ref, ga_ref, x_ref, wo_ref, fw_ref, y_ref):
    mix_p = (jax.nn.silu(gp_ref[0]) * pool_ref[0]).astype(BF16)
    mix_a = (jax.nn.silu(ga_ref[0]) * attn_ref[0]).astype(BF16)
    r = jnp.dot(mix_p, wo_ref[0:POOL_WIDTH, :], preferred_element_type=F32)
    r = r + jnp.dot(mix_a, wo_ref[POOL_WIDTH:, :], preferred_element_type=F32)
    hres = x_ref[0] + r
    ms = jnp.mean(hres * hres, axis=-1, keepdims=True)
    y_ref[0] = hres * lax.rsqrt(ms + NORM_EPS) * fw_ref[...]


def _merge_out(pool_out, g_pool, attn_out, g_attn, x, w_out_bf16, final_norm_w, *, tm):
    b, s, d = x.shape
    assert s % tm == 0
    row = lambda width: pl.BlockSpec((1, tm, width), lambda bi, i: (bi, i, 0))
    vmem = 2 * tm * (2 * POOL_WIDTH + 2 * ATTN_WIDTH + 2 * d) * 4 + 2 * d * d * 2 + 6 * tm * d * 4
    return pl.pallas_call(
        _out_kernel,
        grid=(b, s // tm),
        in_specs=[
            row(POOL_WIDTH), row(POOL_WIDTH), row(ATTN_WIDTH), row(ATTN_WIDTH), row(d),
            pl.BlockSpec((d, d), lambda bi, i: (0, 0)),
            pl.BlockSpec((1, d), lambda bi, i: (0, 0)),
        ],
        out_specs=row(d),
        out_shape=jax.ShapeDtypeStruct((b, s, d), F32),
        compiler_params=pltpu.CompilerParams(
            dimension_semantics=("arbitrary", "arbitrary"),
            vmem_limit_bytes=vmem + 8 * MIB,
        ),
        name="merge_out",
    )(pool_out, g_pool, attn_out, g_attn, x, w_out_bf16, final_norm_w.reshape(1, d))


def _tile(n, target):
    t = min(n, target)
    assert n % t == 0
    return t


def kernel(x_prompt, x_sample, cache_k, cache_v, state_pool, norm_w, w_in, w_pool, pool_scale,
           lambda_q1, lambda_k1, lambda_q2, lambda_k2, subln_w, w_out, final_norm_w):
    depth = norm_w.shape[0]
    assert depth == 1, "the final norm is fused into the single layer's output kernel"
    layer = 0
    lam_init = _lam_init(layer)
    b, s, d = x_prompt.shape
    db, t, _ = x_sample.shape
    past = cache_k.shape[2]

    w_in_b = w_in[layer].astype(BF16)
    w_out_b = w_out[layer].astype(BF16)
    w_pool_b = w_pool[layer].astype(BF16)
    lams = tuple(p[layer].reshape(1, D_HEAD) for p in (lambda_q1, lambda_k1, lambda_q2, lambda_k2))

    u, gp, q, k, v, ga, kb, vt = _project(x_prompt, norm_w[layer], w_in_b, tm=_tile(s, 512), attn_layouts=True)
    hist0 = jnp.zeros((b, HALO, POOL_WIDTH), F32)
    pool_out = _pool_mix(u, hist0, w_pool_b, pool_scale[layer], tm=_tile(s, 512), hist_valid=0)
    attn_out = _prompt_attention(q, kb, vt, lams, subln_w[layer], tq=_tile(s, 512), lam_init=lam_init)
    y_prompt = _merge_out(pool_out, gp, attn_out, ga, x_prompt, w_out_b, final_norm_w, tm=_tile(s, 256))

    xs = x_sample.reshape(1, db * t, d)
    u2, gp2, q2, k2, v2, ga2 = _project(xs, norm_w[layer], w_in_b, tm=_tile(db * t, 512), attn_layouts=False)
    per_stream = lambda a: a.reshape(db, t, a.shape[-1])
    u2, gp2, q2, k2, v2, ga2 = map(per_stream, (u2, gp2, q2, k2, v2, ga2))
    hist = jnp.concatenate([jnp.zeros((db, HALO - POOL_HIST, POOL_WIDTH), F32), state_pool[layer]], axis=1)
    pool_out2 = _pool_mix(u2, hist, w_pool_b, pool_scale[layer], tm=t, hist_valid=POOL_HIST)
    attn2 = _decode_attention(q2, k2, v2, cache_k[layer].reshape(db, past, ATTN_WIDTH),
                              cache_v[layer].reshape(db, past, ATTN_WIDTH), lams, subln_w[layer], lam_init=lam_init)
    flat = lambda a: a.reshape(1, db * t, a.shape[-1])
    y_sample = _merge_out(flat(pool_out2), flat(gp2), flat(attn2), flat(ga2), xs, w_out_b, final_norm_w,
                          tm=_tile(db * t, 256)).reshape(db, t, d)

    heads = lambda a: a.reshape(1, a.shape[0], a.shape[1], N_HEADS, V_HEAD)
    pool_prompt = _last_rows(hist0[:, 1:], u)[None]
    pool_sample = _last_rows(state_pool[layer], u2)[None]
    return (y_prompt, y_sample, heads(k), heads(v), pool_prompt, heads(k2), heads(v2), pool_sample)


def _last_rows(hist, u):
    if u.shape[1] >= POOL_HIST:
        return u[:, u.shape[1] - POOL_HIST:]
    return jnp.concatenate([hist, u], axis=1)[:, -POOL_HIST:]
```

```python
import functools
import math

import jax
import jax.numpy as jnp
import numpy as np
from jax import lax
from jax.experimental import pallas as pl
from jax.experimental.pallas import tpu as pltpu

F32 = jnp.float32
BF16 = jnp.bfloat16

D_MODEL = 2048
CHUNK = 64
CHUNK_SHIFT = 6
POOL_WINDOWS = (2, 4, 8, 16)
POOL_GROUP_DIM = 128
POOL_WIDTH = len(POOL_WINDOWS) * POOL_GROUP_DIM
POOL_HIST = max(POOL_WINDOWS) - 1
HALO = POOL_HIST + 1
ATTN_WIDTH = D_MODEL - POOL_WIDTH
N_HEADS = 12
V_HEAD = ATTN_WIDTH // N_HEADS
D_HEAD = V_HEAD // 2
IN_WIDTH = 2 * POOL_WIDTH + 4 * ATTN_WIDTH
NORM_EPS = 1e-6
SUBLN_EPS = 1e-5
MASK_VALUE = -1e30
LOG2E = math.log2(math.e)
Q_SCALE = (D_HEAD ** -0.5) * LOG2E

COL_TILE = 512
N_COL_TILES = IN_WIDTH // COL_TILE
ATTN_COL_TILES = ATTN_WIDTH // COL_TILE
HEADS_PER_TILE = COL_TILE // V_HEAD
SEG_GP, SEG_Q, SEG_K, SEG_V, SEG_GA = 1, 2, 2 + ATTN_COL_TILES, 2 + 2 * ATTN_COL_TILES, 2 + 3 * ATTN_COL_TILES

BF16_SUBLANES = 16
MIB = 1024 * 1024


def _alibi_sigmas():
    def pow2_slopes(m):
        start = 2.0 ** (-8.0 / m)
        return [start ** (i + 1) for i in range(m)]

    n = N_HEADS
    if math.log2(n).is_integer():
        s = pow2_slopes(n)
    else:
        c = 2 ** int(math.floor(math.log2(n)))
        s = pow2_slopes(c) + pow2_slopes(2 * c)[0::2][: n - c]
    return jnp.asarray(np.array(s, dtype=np.float32) * np.float32(LOG2E))


def _lam_init(layer):
    return 0.8 - 0.6 * math.exp(-0.3 * layer)


def _lam(lq1_ref, lk1_ref, lq2_ref, lk2_ref, lam_init):
    a = jnp.sum(lq1_ref[...] * lk1_ref[...], axis=-1, keepdims=True)
    b = jnp.sum(lq2_ref[...] * lk2_ref[...], axis=-1, keepdims=True)
    return jnp.exp(a) - jnp.exp(b) + lam_init


def _proj_kernel(x_ref, nw_ref, w_ref, u_ref, gp_ref, q_ref, k_ref, v_ref, ga_ref, *rest, attn_layouts):
    if attn_layouts:
        kb_ref, vt_ref, h_scr = rest
    else:
        (h_scr,) = rest
    j = pl.program_id(2)

    @pl.when(j == 0)
    def _():
        x = x_ref[0]
        ms = jnp.mean(x * x, axis=-1, keepdims=True)
        h_scr[...] = (x * lax.rsqrt(ms + NORM_EPS) * nw_ref[...]).astype(BF16)

    z = jnp.dot(h_scr[...], w_ref[...], preferred_element_type=F32)

    def head_slabs():
        return [(hh, z[:, hh * V_HEAD:(hh + 1) * V_HEAD]) for hh in range(HEADS_PER_TILE)]

    @pl.when(j == 0)
    def _():
        u_ref[0] = z

    @pl.when(j == SEG_GP)
    def _():
        gp_ref[0] = z

    @pl.when((j >= SEG_Q) & (j < SEG_K))
    def _():
        for hh, zh in head_slabs():
            q_ref[0, hh] = (zh * Q_SCALE).astype(BF16)

    @pl.when((j >= SEG_K) & (j < SEG_V))
    def _():
        for hh, zh in head_slabs():
            k_ref[0, hh] = zh
            if attn_layouts:
                kb_ref[0, hh] = zh.astype(BF16)

    @pl.when((j >= SEG_V) & (j < SEG_GA))
    def _():
        for hh, zh in head_slabs():
            v_ref[0, hh] = zh
            if attn_layouts:
                vt_ref[0, hh] = zh.T.astype(BF16)

    @pl.when(j >= SEG_GA)
    def _():
        ga_ref[0] = z


def _project(x, norm_w, w_in_bf16, *, tm, attn_layouts):
    b, s, d = x.shape
    assert s % tm == 0 and d == D_MODEL
    grid = (b, s // tm, N_COL_TILES)

    def seg(first, j):
        return jnp.clip(j - first, 0, ATTN_COL_TILES - 1)

    row_tile = lambda bi, i, j: (bi, i, 0)
    head_block = (1, HEADS_PER_TILE, tm, V_HEAD)
    head_map = lambda first: (lambda bi, i, j: (bi, seg(first, j), i, 0))
    out_shape = [
        jax.ShapeDtypeStruct((b, s, POOL_WIDTH), F32),
        jax.ShapeDtypeStruct((b, s, POOL_WIDTH), F32),
        jax.ShapeDtypeStruct((b, N_HEADS, s, V_HEAD), BF16),
        jax.ShapeDtypeStruct((b, N_HEADS, s, V_HEAD), F32),
        jax.ShapeDtypeStruct((b, N_HEADS, s, V_HEAD), F32),
        jax.ShapeDtypeStruct((b, s, ATTN_WIDTH), F32),
    ]
    out_specs = [
        pl.BlockSpec((1, tm, COL_TILE), row_tile),
        pl.BlockSpec((1, tm, COL_TILE), row_tile),
        pl.BlockSpec(head_block, head_map(SEG_Q)),
        pl.BlockSpec(head_block, head_map(SEG_K)),
        pl.BlockSpec(head_block, head_map(SEG_V)),
        pl.BlockSpec((1, tm, COL_TILE), lambda bi, i, j: (bi, i, seg(SEG_GA, j))),
    ]
    if attn_layouts:
        out_shape += [
            jax.ShapeDtypeStruct((b, N_HEADS, s, V_HEAD), BF16),
            jax.ShapeDtypeStruct((b, N_HEADS, V_HEAD, s), BF16),
        ]
        out_specs += [
            pl.BlockSpec(head_block, head_map(SEG_K)),
            pl.BlockSpec((1, HEADS_PER_TILE, V_HEAD, tm), lambda bi, i, j: (bi, seg(SEG_V, j), 0, i)),
        ]
    vmem = 2 * tm * d * 4 + tm * d * 2 + 2 * d * COL_TILE * 2 + 2 * tm * COL_TILE * (5 * 4 + 3 * 2) + 4 * tm * COL_TILE * 4
    return pl.pallas_call(
        functools.partial(_proj_kernel, attn_layouts=attn_layouts),
        grid=grid,
        in_specs=[
            pl.BlockSpec((1, tm, d), row_tile),
            pl.BlockSpec((1, d), lambda bi, i, j: (0, 0)),
            pl.BlockSpec((d, COL_TILE), lambda bi, i, j: (0, j)),
        ],
        out_specs=out_specs,
        out_shape=out_shape,
        scratch_shapes=[pltpu.VMEM((tm, d), BF16)],
        compiler_params=pltpu.CompilerParams(
            dimension_semantics=("arbitrary", "arbitrary", "arbitrary"),
            vmem_limit_bytes=vmem + 8 * MIB,
        ),
        name="proj",
    )(x, norm_w.reshape(1, d), w_in_bf16)


def _pool_kernel(u_ref, prev_ref, hist_ref, wp_ref, ps_ref, o_ref, ext_scr, *, tm, hist_valid):
    i = pl.program_id(1)
    ext_scr[0:HALO, :] = jnp.where(i == 0, hist_ref[0], prev_ref[0])
    ext_scr[HALO:HALO + tm, :] = u_ref[0]
    frame = i * tm + lax.broadcasted_iota(jnp.int32, (tm, POOL_GROUP_DIM), 0)
    for g, w in enumerate(POOL_WINDOWS):
        lanes = slice(g * POOL_GROUP_DIM, (g + 1) * POOL_GROUP_DIM)
        cur = ext_scr[HALO:HALO + tm, lanes]
        acc = cur
        for dlt in range(1, w):
            acc = acc + ext_scr[HALO - dlt:HALO - dlt + tm, lanes]
        count = jnp.minimum(w, hist_valid + frame + 1).astype(F32)
        diff = acc / count - cur
        y = jnp.dot(diff.astype(BF16), wp_ref[g], preferred_element_type=F32)
        o_ref[0, :, lanes] = y * ps_ref[:, lanes]


def _pool_mix(u, hist, w_pool_bf16, pool_scale, *, tm, hist_valid):
    b, t, c = u.shape
    assert t % tm == 0 and tm % HALO == 0
    blocks_per_tile = tm // HALO
    return pl.pallas_call(
        functools.partial(_pool_kernel, tm=tm, hist_valid=hist_valid),
        grid=(b, t // tm),
        in_specs=[
            pl.BlockSpec((1, tm, c), lambda bi, i: (bi, i, 0)),
            pl.BlockSpec((1, HALO, c), lambda bi, i: (bi, jnp.maximum(i * blocks_per_tile - 1, 0), 0)),
            pl.BlockSpec((1, HALO, c), lambda bi, i: (bi, 0, 0)),
            pl.BlockSpec((len(POOL_WINDOWS), POOL_GROUP_DIM, POOL_GROUP_DIM), lambda bi, i: (0, 0, 0)),
            pl.BlockSpec((1, c), lambda bi, i: (0, 0)),
        ],
        out_specs=pl.BlockSpec((1, tm, c), lambda bi, i: (bi, i, 0)),
        out_shape=jax.ShapeDtypeStruct((b, t, c), F32),
        scratch_shapes=[pltpu.VMEM((HALO + tm, c), F32)],
        compiler_params=pltpu.CompilerParams(dimension_semantics=("arbitrary", "arbitrary")),
        name="pool",
    )(u, u, hist, w_pool_bf16, pool_scale.reshape(1, c))


ACC_ROWS = V_HEAD + BF16_SUBLANES


def _attn_kernel(sig_ref, q_ref, k_ref, vt_ref, lq1_ref, lk1_ref, lq2_ref, lk2_ref, sw_ref, o_ref,
                 qp_scr, bias_scr, sh_scr, mb_scr, m_scr, acc_scr, *, tq, lam_init):
    h = pl.program_id(1)
    qi = pl.program_id(2)
    sigma = sig_ref[h]
    decay = sigma * tq

    @pl.when(qi == 0)
    def _():
        kj = lax.broadcasted_iota(jnp.int32, (tq, tq), 0)
        qq = lax.broadcasted_iota(jnp.int32, (tq, tq), 1)
        rel = (qq - kj).astype(F32)
        bias_scr[0] = -sigma * rel
        allowed = lax.shift_right_logical(kj, CHUNK_SHIFT) <= lax.shift_right_logical(qq, CHUNK_SHIFT)
        bias_scr[1] = jnp.where(allowed, -sigma * jnp.abs(rel), MASK_VALUE)

    qt = q_ref[0, 0].astype(F32).T
    row = lax.broadcasted_iota(jnp.int32, (V_HEAD, tq), 0)
    qp_scr[0] = jnp.where(row < D_HEAD, qt, 0.0).astype(BF16)
    qp_scr[1] = jnp.where(row >= D_HEAD, qt, 0.0).astype(BF16)
    m_scr[...] = jnp.full(m_scr.shape, MASK_VALUE, F32)
    acc_scr[...] = jnp.zeros(acc_scr.shape, F32)
    ones_rows = jnp.ones((BF16_SUBLANES, tq), BF16)

    def step(value_kb, score_kb):
        if score_kb is not None:
            k0 = pl.multiple_of(score_kb * tq, tq)
            kblk = k_ref[0, 0, pl.ds(k0, tq), :]
            sel = (score_kb == qi).astype(jnp.int32)
        if value_kb is not None:
            v0 = pl.multiple_of(value_kb * tq, tq)
            vblk = jnp.concatenate([vt_ref[0, 0, :, pl.ds(v0, tq)], ones_rows], axis=0)
        for c in range(2):
            if value_kb is not None:
                m_old = m_scr[c] - decay
                m_new = jnp.maximum(m_old, mb_scr[c])
                p = jnp.exp2(sh_scr[c] - m_new).astype(BF16)
                alpha = jnp.exp2(m_old - m_new)
            if score_kb is not None:
                sh = jnp.dot(kblk, qp_scr[c], preferred_element_type=F32) + bias_scr[sel]
                sh_scr[c] = sh
                mb_scr[c] = jnp.max(sh, axis=0, keepdims=True)
            if value_kb is not None:
                acc_scr[c] = alpha * acc_scr[c] + jnp.dot(vblk, p, preferred_element_type=F32)
                m_scr[c] = m_new

    step(None, 0)

    def body(t, carry):
        step(t, t + 1)
        return carry

    lax.fori_loop(0, qi, body, 0)
    step(qi, None)

    lam = _lam(lq1_ref, lk1_ref, lq2_ref, lk2_ref, lam_init)
    o1 = acc_scr[0, 0:V_HEAD, :] / acc_scr[0, V_HEAD:V_HEAD + 1, :]
    o2 = acc_scr[1, 0:V_HEAD, :] / acc_scr[1, V_HEAD:V_HEAD + 1, :]
    o = o1 - lam * o2
    o = o * lax.rsqrt(jnp.mean(o * o, axis=0, keepdims=True) + SUBLN_EPS)
    o_ref[0] = o.T * sw_ref[...] * (1.0 - lam_init)


def _prompt_attention(q, kb, vt, lams, subln_w, *, tq, lam_init):
    b, _, s, _ = q.shape
    assert s % tq == 0 and tq % CHUNK == 0
    lam_spec = pl.BlockSpec((1, D_HEAD), lambda bi, h, i: (0, 0))
    vmem = 2 * 2 * s * V_HEAD * 2 + (2 + 2) * tq * tq * 4 + 8 * tq * tq * 4 + 8 * tq * ACC_ROWS * 4
    return pl.pallas_call(
        functools.partial(_attn_kernel, tq=tq, lam_init=lam_init),
        grid=(b, N_HEADS, s // tq),
        in_specs=[
            pl.BlockSpec(memory_space=pltpu.SMEM),
            pl.BlockSpec((1, 1, tq, V_HEAD), lambda bi, h, i: (bi, h, i, 0)),
            pl.BlockSpec((1, 1, s, V_HEAD), lambda bi, h, i: (bi, h, 0, 0)),
            pl.BlockSpec((1, 1, V_HEAD, s), lambda bi, h, i: (bi, h, 0, 0)),
            lam_spec, lam_spec, lam_spec, lam_spec,
            pl.BlockSpec((1, V_HEAD), lambda bi, h, i: (0, 0)),
        ],
        out_specs=pl.BlockSpec((1, tq, V_HEAD), lambda bi, h, i: (bi, i, h)),
        out_shape=jax.ShapeDtypeStruct((b, s, ATTN_WIDTH), F32),
        scratch_shapes=[
            pltpu.VMEM((2, V_HEAD, tq), BF16),
            pltpu.VMEM((2, tq, tq), F32),
            pltpu.VMEM((2, tq, tq), F32),
            pltpu.VMEM((2, 1, tq), F32),
            pltpu.VMEM((2, 1, tq), F32),
            pltpu.VMEM((2, ACC_ROWS, tq), F32),
        ],
        compiler_params=pltpu.CompilerParams(
            dimension_semantics=("arbitrary", "arbitrary", "arbitrary"),
            vmem_limit_bytes=vmem + 8 * MIB,
        ),
        name="attn",
    )(_alibi_sigmas(), q, kb, vt, *lams, subln_w.reshape(1, V_HEAD))


def _decode_kernel(sig_ref, q_ref, kn_ref, vn_ref, kc_ref, vc_ref, lq1_ref, lk1_ref, lq2_ref, lk2_ref, sw_ref,
                   o_ref, *, t, past, lam_init):
    hg = pl.program_id(1)
    lam = _lam(lq1_ref, lk1_ref, lq2_ref, lk2_ref, lam_init)
    qpos_c = past + lax.broadcasted_iota(jnp.int32, (t, past), 0)
    kpos_c = lax.broadcasted_iota(jnp.int32, (t, past), 1)
    qpos_n = past + lax.broadcasted_iota(jnp.int32, (t, t), 0)
    kpos_n = past + lax.broadcasted_iota(jnp.int32, (t, t), 1)

    def bias(qpos, kpos, sigma):
        allowed = lax.shift_right_logical(kpos, CHUNK_SHIFT) <= lax.shift_right_logical(qpos, CHUNK_SHIFT)
        return jnp.where(allowed, -sigma * jnp.abs(qpos - kpos).astype(F32), MASK_VALUE)

    lane = lax.broadcasted_iota(jnp.int32, (t, V_HEAD), 1)
    nt = (((1,), (1,)), ((), ()))
    for hh in range(HEADS_PER_TILE):
        sigma = sig_ref[hg * HEADS_PER_TILE + hh]
        cols = slice(hh * V_HEAD, (hh + 1) * V_HEAD)
        q = q_ref[0, hh]
        kc = kc_ref[0, hh].astype(BF16)
        vc = vc_ref[0, hh].astype(BF16)
        kn = kn_ref[0, hh].astype(BF16)
        vn = vn_ref[0, hh].astype(BF16)
        bias_c = bias(qpos_c, kpos_c, sigma)
        bias_n = bias(qpos_n, kpos_n, sigma)
        outs = []
        for c in range(2):
            keep = (lane < D_HEAD) if c == 0 else (lane >= D_HEAD)
            qc = jnp.where(keep, q, jnp.zeros_like(q))
            s_c = lax.dot_general(qc, kc, nt, preferred_element_type=F32) + bias_c
            s_n = lax.dot_general(qc, kn, nt, preferred_element_type=F32) + bias_n
            m = jnp.maximum(jnp.max(s_c, axis=-1, keepdims=True), jnp.max(s_n, axis=-1, keepdims=True))
            p_c = jnp.exp2(s_c - m)
            p_n = jnp.exp2(s_n - m)
            l = jnp.sum(p_c, axis=-1, keepdims=True) + jnp.sum(p_n, axis=-1, keepdims=True)
            pv = (jnp.dot(p_c.astype(BF16), vc, preferred_element_type=F32)
                  + jnp.dot(p_n.astype(BF16), vn, preferred_element_type=F32))
            outs.append(pv / l)
        o = outs[0] - lam * outs[1]
        o = o * lax.rsqrt(jnp.mean(o * o, axis=-1, keepdims=True) + SUBLN_EPS)
        o_ref[0, :, cols] = o * sw_ref[...] * (1.0 - lam_init)


def _decode_attention(q, k_new, v_new, cache_k, cache_v, lams, subln_w, *, db, lam_init):
    rows = q.shape[2]
    t = rows // db
    past = cache_k.shape[2]
    new_spec = pl.BlockSpec((1, HEADS_PER_TILE, t, V_HEAD), lambda bi, g: (0, g, bi, 0))
    cache_spec = pl.BlockSpec((1, HEADS_PER_TILE, past, V_HEAD), lambda bi, g: (bi, g, 0, 0))
    lam_spec = pl.BlockSpec((1, D_HEAD), lambda bi, g: (0, 0))
    vmem = 2 * 2 * past * COL_TILE * 4 + 8 * past * V_HEAD * 4 + 16 * t * past * 4
    return pl.pallas_call(
        functools.partial(_decode_kernel, t=t, past=past, lam_init=lam_init),
        grid=(db, N_HEADS // HEADS_PER_TILE),
        in_specs=[
            pl.BlockSpec(memory_space=pltpu.SMEM),
            new_spec, new_spec, new_spec, cache_spec, cache_spec,
            lam_spec, lam_spec, lam_spec, lam_spec,
            pl.BlockSpec((1, V_HEAD), lambda bi, g: (0, 0)),
        ],
        out_specs=pl.BlockSpec((1, t, COL_TILE), lambda bi, g: (bi, 0, g)),
        out_shape=jax.ShapeDtypeStruct((db, t, ATTN_WIDTH), F32),
        compiler_params=pltpu.CompilerParams(
            dimension_semantics=("arbitrary", "arbitrary"),
            vmem_limit_bytes=vmem + 8 * MIB,
        ),
        name="decode_attn",
    )(_alibi_sigmas(), q, k_new, v_new, cache_k, cache_v, *lams, subln_w.reshape(1, V_HEAD))


def _out_kernel(pool_ref, gp_ref, attn_ref, ga_ref, x_ref, wo_ref, fw_ref, y_ref):
    mix_p = (jax.nn.silu(gp_ref[0]) * pool_ref[0]).astype(BF16)
    mix_a = (jax.nn.silu(ga_ref[0]) * attn_ref[0]).astype(BF16)
    r = jnp.dot(mix_p, wo_ref[0:POOL_WIDTH, :], preferred_element_type=F32)
    r = r + jnp.dot(mix_a, wo_ref[POOL_WIDTH:, :], preferred_element_type=F32)
    hres = x_ref[0] + r
    ms = jnp.mean(hres * hres, axis=-1, keepdims=True)
    y_ref[0] = hres * lax.rsqrt(ms + NORM_EPS) * fw_ref[...]


def _merge_out(pool_out, g_pool, attn_out, g_attn, x, w_out_bf16, final_norm_w, *, tm):
    b, s, d = x.shape
    assert s % tm == 0
    row = lambda width: pl.BlockSpec((1, tm, width), lambda bi, i: (bi, i, 0))
    vmem = 2 * tm * (2 * POOL_WIDTH + 2 * ATTN_WIDTH + 2 * d) * 4 + 2 * d * d * 2 + 6 * tm * d * 4
    return pl.pallas_call(
        _out_kernel,
        grid=(b, s // tm),
        in_specs=[
            row(POOL_WIDTH), row(POOL_WIDTH), row(ATTN_WIDTH), row(ATTN_WIDTH), row(d),
            pl.BlockSpec((d, d), lambda bi, i: (0, 0)),
            pl.BlockSpec((1, d), lambda bi, i: (0, 0)),
        ],
        out_specs=row(d),
        out_shape=jax.ShapeDtypeStruct((b, s, d), F32),
        compiler_params=pltpu.CompilerParams(
            dimension_semantics=("arbitrary", "arbitrary"),
            vmem_limit_bytes=vmem + 8 * MIB,
        ),
        name="merge_out",
    )(pool_out, g_pool, attn_out, g_attn, x, w_out_bf16, final_norm_w.reshape(1, d))


def _tile(n, target):
    t = min(n, target)
    assert n % t == 0
    return t


def _last_rows(hist, u):
    if u.shape[1] >= POOL_HIST:
        return u[:, u.shape[1] - POOL_HIST:]
    return jnp.concatenate([hist, u], axis=1)[:, -POOL_HIST:]


def kernel(x_prompt, x_sample, cache_k, cache_v, state_pool, norm_w, w_in, w_pool, pool_scale,
           lambda_q1, lambda_k1, lambda_q2, lambda_k2, subln_w, w_out, final_norm_w):
    depth = norm_w.shape[0]
    assert depth == 1, "the final norm is fused into the single layer's output kernel"
    layer = 0
    lam_init = _lam_init(layer)
    b, s, d = x_prompt.shape
    db, t, _ = x_sample.shape

    w_in_b = w_in[layer].astype(BF16)
    w_out_b = w_out[layer].astype(BF16)
    w_pool_b = w_pool[layer].astype(BF16)
    lams = tuple(p[layer].reshape(1, D_HEAD) for p in (lambda_q1, lambda_k1, lambda_q2, lambda_k2))
    position_major = lambda a: jnp.transpose(a, (0, 2, 1, 3))

    u, gp, q, k, v, ga, kb, vt = _project(x_prompt, norm_w[layer], w_in_b, tm=_tile(s, 512), attn_layouts=True)
    hist0 = jnp.zeros((b, HALO, POOL_WIDTH), F32)
    pool_out = _pool_mix(u, hist0, w_pool_b, pool_scale[layer], tm=_tile(s, 512), hist_valid=0)
    attn_out = _prompt_attention(q, kb, vt, lams, subln_w[layer], tq=_tile(s, 512), lam_init=lam_init)
    y_prompt = _merge_out(pool_out, gp, attn_out, ga, x_prompt, w_out_b, final_norm_w, tm=_tile(s, 256))

    xs = x_sample.reshape(1, db * t, d)
    u2, gp2, q2, k2, v2, ga2 = _project(xs, norm_w[layer], w_in_b, tm=_tile(db * t, 512), attn_layouts=False)
    hist = jnp.concatenate([jnp.zeros((db, HALO - POOL_HIST, POOL_WIDTH), F32), state_pool[layer]], axis=1)
    u2s = u2.reshape(db, t, POOL_WIDTH)
    pool_out2 = _pool_mix(u2s, hist, w_pool_b, pool_scale[layer], tm=t, hist_valid=POOL_HIST)
    attn2 = _decode_attention(q2, k2, v2, position_major(cache_k[layer]), position_major(cache_v[layer]),
                              lams, subln_w[layer], db=db, lam_init=lam_init)
    flat = lambda a: a.reshape(1, db * t, a.shape[-1])
    y_sample = _merge_out(flat(pool_out2), gp2, flat(attn2), ga2, xs, w_out_b, final_norm_w,
                          tm=_tile(db * t, 256)).reshape(db, t, d)

    per_stream = lambda a: position_major(a).reshape(db, t, N_HEADS, V_HEAD)[None]
    return (y_prompt, y_sample, position_major(k)[None], position_major(v)[None], _last_rows(hist0[:, 1:], u)[None],
            per_stream(k2), per_stream(v2), _last_rows(state_pool[layer], u2s)[None])
```

```python
import functools
import math

import jax
import jax.numpy as jnp
import numpy as np
from jax import lax
from jax.experimental import pallas as pl
from jax.experimental.pallas import tpu as pltpu

F32 = jnp.float32
BF16 = jnp.bfloat16

D_MODEL = 2048
CHUNK = 64
CHUNK_SHIFT = 6
POOL_WINDOWS = (2, 4, 8, 16)
POOL_GROUP_DIM = 128
POOL_WIDTH = len(POOL_WINDOWS) * POOL_GROUP_DIM
POOL_HIST = max(POOL_WINDOWS) - 1
HALO = POOL_HIST + 1
ATTN_WIDTH = D_MODEL - POOL_WIDTH
N_HEADS = 12
V_HEAD = ATTN_WIDTH // N_HEADS
D_HEAD = V_HEAD // 2
IN_WIDTH = 2 * POOL_WIDTH + 4 * ATTN_WIDTH
NORM_EPS = 1e-6
SUBLN_EPS = 1e-5
MASK_VALUE = -1e30
LOG2E = math.log2(math.e)
Q_SCALE = (D_HEAD ** -0.5) * LOG2E

COL_TILE = 512
N_COL_TILES = IN_WIDTH // COL_TILE
ATTN_COL_TILES = ATTN_WIDTH // COL_TILE
HEADS_PER_TILE = COL_TILE // V_HEAD
SEG_GP, SEG_Q, SEG_K, SEG_V, SEG_GA = 1, 2, 2 + ATTN_COL_TILES, 2 + 2 * ATTN_COL_TILES, 2 + 3 * ATTN_COL_TILES

BF16_SUBLANES = 16
MIB = 1024 * 1024


def _alibi_sigmas():
    def pow2_slopes(m):
        start = 2.0 ** (-8.0 / m)
        return [start ** (i + 1) for i in range(m)]

    n = N_HEADS
    if math.log2(n).is_integer():
        s = pow2_slopes(n)
    else:
        c = 2 ** int(math.floor(math.log2(n)))
        s = pow2_slopes(c) + pow2_slopes(2 * c)[0::2][: n - c]
    return jnp.asarray(np.array(s, dtype=np.float32) * np.float32(LOG2E))


def _lam_init(layer):
    return 0.8 - 0.6 * math.exp(-0.3 * layer)


def _lam(lq1_ref, lk1_ref, lq2_ref, lk2_ref, lam_init):
    a = jnp.sum(lq1_ref[...] * lk1_ref[...], axis=-1, keepdims=True)
    b = jnp.sum(lq2_ref[...] * lk2_ref[...], axis=-1, keepdims=True)
    return jnp.exp(a) - jnp.exp(b) + lam_init


def _proj_kernel(x_ref, nw_ref, w_ref, u_ref, gp_ref, q_ref, k_ref, v_ref, ga_ref, *rest, attn_layouts):
    if attn_layouts:
        kb_ref, vt_ref, h_scr = rest
    else:
        (h_scr,) = rest
    j = pl.program_id(2)

    @pl.when(j == 0)
    def _():
        x = x_ref[0]
        ms = jnp.mean(x * x, axis=-1, keepdims=True)
        h_scr[...] = (x * lax.rsqrt(ms + NORM_EPS) * nw_ref[...]).astype(BF16)

    def tile():
        return jnp.dot(h_scr[...], w_ref[...], preferred_element_type=F32)

    @pl.when(j == 0)
    def _():
        u_ref[0] = tile()

    @pl.when(j == SEG_GP)
    def _():
        gp_ref[0] = tile()

    @pl.when((j >= SEG_Q) & (j < SEG_K))
    def _():
        z = tile()
        for hh in range(HEADS_PER_TILE):
            q_ref[0, hh] = (z[:, hh * V_HEAD:(hh + 1) * V_HEAD] * Q_SCALE).astype(BF16)

    @pl.when((j >= SEG_K) & (j < SEG_V))
    def _():
        z = tile()
        for hh in range(HEADS_PER_TILE):
            zh = z[:, hh * V_HEAD:(hh + 1) * V_HEAD]
            k_ref[0, hh] = zh
            if attn_layouts:
                kb_ref[0, hh] = zh.astype(BF16)

    @pl.when((j >= SEG_V) & (j < SEG_GA))
    def _():
        z = tile()
        for hh in range(HEADS_PER_TILE):
            zh = z[:, hh * V_HEAD:(hh + 1) * V_HEAD]
            v_ref[0, hh] = zh
            if attn_layouts:
                vt_ref[0, hh] = zh.T.astype(BF16)

    @pl.when(j >= SEG_GA)
    def _():
        ga_ref[0] = tile()


def _project(x, norm_w, w_in_bf16, *, tm, attn_layouts):
    b, s, d = x.shape
    assert s % tm == 0 and d == D_MODEL
    grid = (b, s // tm, N_COL_TILES)

    def seg(first, j):
        return jnp.clip(j - first, 0, ATTN_COL_TILES - 1)

    row_tile = lambda bi, i, j: (bi, i, 0)
    head_block = (1, HEADS_PER_TILE, tm, V_HEAD)
    head_map = lambda first: (lambda bi, i, j: (bi, seg(first, j), i, 0))
    out_shape = [
        jax.ShapeDtypeStruct((b, s, POOL_WIDTH), F32),
        jax.ShapeDtypeStruct((b, s, POOL_WIDTH), F32),
        jax.ShapeDtypeStruct((b, N_HEADS, s, V_HEAD), BF16),
        jax.ShapeDtypeStruct((b, N_HEADS, s, V_HEAD), F32),
        jax.ShapeDtypeStruct((b, N_HEADS, s, V_HEAD), F32),
        jax.ShapeDtypeStruct((b, s, ATTN_WIDTH), F32),
    ]
    out_specs = [
        pl.BlockSpec((1, tm, COL_TILE), row_tile),
        pl.BlockSpec((1, tm, COL_TILE), row_tile),
        pl.BlockSpec(head_block, head_map(SEG_Q)),
        pl.BlockSpec(head_block, head_map(SEG_K)),
        pl.BlockSpec(head_block, head_map(SEG_V)),
        pl.BlockSpec((1, tm, COL_TILE), lambda bi, i, j: (bi, i, seg(SEG_GA, j))),
    ]
    if attn_layouts:
        out_shape += [
            jax.ShapeDtypeStruct((b, N_HEADS, s, V_HEAD), BF16),
            jax.ShapeDtypeStruct((b, N_HEADS, V_HEAD, s), BF16),
        ]
        out_specs += [
            pl.BlockSpec(head_block, head_map(SEG_K)),
            pl.BlockSpec((1, HEADS_PER_TILE, V_HEAD, tm), lambda bi, i, j: (bi, seg(SEG_V, j), 0, i)),
        ]
    vmem = 2 * tm * d * 4 + tm * d * 2 + 2 * d * COL_TILE * 2 + 2 * tm * COL_TILE * (5 * 4 + 3 * 2) + 4 * tm * COL_TILE * 4
    return pl.pallas_call(
        functools.partial(_proj_kernel, attn_layouts=attn_layouts),
        grid=grid,
        in_specs=[
            pl.BlockSpec((1, tm, d), row_tile),
            pl.BlockSpec((1, d), lambda bi, i, j: (0, 0)),
            pl.BlockSpec((d, COL_TILE), lambda bi, i, j: (0, j)),
        ],
        out_specs=out_specs,
        out_shape=out_shape,
        scratch_shapes=[pltpu.VMEM((tm, d), BF16)],
        compiler_params=pltpu.CompilerParams(
            dimension_semantics=("arbitrary", "arbitrary", "arbitrary"),
            vmem_limit_bytes=vmem + 8 * MIB,
        ),
        name="proj",
    )(x, norm_w.reshape(1, d), w_in_bf16)


def _pool_kernel(u_ref, prev_ref, hist_ref, wp_ref, ps_ref, o_ref, ext_scr, *, tm, hist_valid):
    i = pl.program_id(1)
    ext_scr[0:HALO, :] = jnp.where(i == 0, hist_ref[0], prev_ref[0])
    ext_scr[HALO:HALO + tm, :] = u_ref[0]
    frame = i * tm + lax.broadcasted_iota(jnp.int32, (tm, POOL_GROUP_DIM), 0)
    for g, w in enumerate(POOL_WINDOWS):
        lanes = slice(g * POOL_GROUP_DIM, (g + 1) * POOL_GROUP_DIM)
        cur = ext_scr[HALO:HALO + tm, lanes]
        acc = cur
        for dlt in range(1, w):
            acc = acc + ext_scr[HALO - dlt:HALO - dlt + tm, lanes]
        count = jnp.minimum(w, hist_valid + frame + 1).astype(F32)
        diff = acc / count - cur
        y = jnp.dot(diff.astype(BF16), wp_ref[g], preferred_element_type=F32)
        o_ref[0, :, lanes] = y * ps_ref[:, lanes]


def _pool_mix(u, hist, w_pool_bf16, pool_scale, *, tm, hist_valid):
    b, t, c = u.shape
    assert t % tm == 0 and tm % HALO == 0
    blocks_per_tile = tm // HALO
    return pl.pallas_call(
        functools.partial(_pool_kernel, tm=tm, hist_valid=hist_valid),
        grid=(b, t // tm),
        in_specs=[
            pl.BlockSpec((1, tm, c), lambda bi, i: (bi, i, 0)),
            pl.BlockSpec((1, HALO, c), lambda bi, i: (bi, jnp.maximum(i * blocks_per_tile - 1, 0), 0)),
            pl.BlockSpec((1, HALO, c), lambda bi, i: (bi, 0, 0)),
            pl.BlockSpec((len(POOL_WINDOWS), POOL_GROUP_DIM, POOL_GROUP_DIM), lambda bi, i: (0, 0, 0)),
            pl.BlockSpec((1, c), lambda bi, i: (0, 0)),
        ],
        out_specs=pl.BlockSpec((1, tm, c), lambda bi, i: (bi, i, 0)),
        out_shape=jax.ShapeDtypeStruct((b, t, c), F32),
        scratch_shapes=[pltpu.VMEM((HALO + tm, c), F32)],
        compiler_params=pltpu.CompilerParams(dimension_semantics=("arbitrary", "arbitrary")),
        name="pool",
    )(u, u, hist, w_pool_bf16, pool_scale.reshape(1, c))


ACC_ROWS = V_HEAD + BF16_SUBLANES
KEY_BLOCK_UNROLL = 4


def _attn_kernel(sig_ref, q_ref, k_ref, vt_ref, lq1_ref, lk1_ref, lq2_ref, lk2_ref, sw_ref, o_ref,
                 qp_scr, bias_scr, sh_scr, mb_scr, m_scr, acc_scr, *, tq, lam_init):
    h = pl.program_id(1)
    qi = pl.program_id(2)
    sigma = sig_ref[h]
    decay = sigma * tq

    @pl.when(qi == 0)
    def _():
        kj = lax.broadcasted_iota(jnp.int32, (tq, tq), 0)
        qq = lax.broadcasted_iota(jnp.int32, (tq, tq), 1)
        rel = (qq - kj).astype(F32)
        bias_scr[0] = -sigma * rel
        allowed = lax.shift_right_logical(kj, CHUNK_SHIFT) <= lax.shift_right_logical(qq, CHUNK_SHIFT)
        bias_scr[1] = jnp.where(allowed, -sigma * jnp.abs(rel), MASK_VALUE)

    qt = q_ref[0, 0].astype(F32).T
    row = lax.broadcasted_iota(jnp.int32, (V_HEAD, tq), 0)
    qp_scr[0] = jnp.where(row < D_HEAD, qt, 0.0).astype(BF16)
    qp_scr[1] = jnp.where(row >= D_HEAD, qt, 0.0).astype(BF16)
    m_scr[...] = jnp.full(m_scr.shape, MASK_VALUE, F32)
    acc_scr[...] = jnp.zeros(acc_scr.shape, F32)
    ones_rows = jnp.ones((BF16_SUBLANES, tq), BF16)

    def step(value_kb, score_kb):
        if score_kb is not None:
            k0 = pl.multiple_of(score_kb * tq, tq)
            kblk = k_ref[0, 0, pl.ds(k0, tq), :]
            sel = (score_kb == qi).astype(jnp.int32)
        if value_kb is not None:
            v0 = pl.multiple_of(value_kb * tq, tq)
            vblk = jnp.concatenate([vt_ref[0, 0, :, pl.ds(v0, tq)], ones_rows], axis=0)
        for c in range(2):
            if value_kb is not None:
                m_old = m_scr[c] - decay
                m_new = jnp.maximum(m_old, mb_scr[c])
                p = jnp.exp2(sh_scr[c] - m_new).astype(BF16)
                alpha = jnp.exp2(m_old - m_new)
            if score_kb is not None:
                sh = jnp.dot(kblk, qp_scr[c], preferred_element_type=F32) + bias_scr[sel]
                sh_scr[c] = sh
                mb_scr[c] = jnp.max(sh, axis=0, keepdims=True)
            if value_kb is not None:
                acc_scr[c] = alpha * acc_scr[c] + jnp.dot(vblk, p, preferred_element_type=F32)
                m_scr[c] = m_new

    step(None, 0)

    def unrolled_body(i, carry):
        for u in range(KEY_BLOCK_UNROLL):
            step(i * KEY_BLOCK_UNROLL + u, i * KEY_BLOCK_UNROLL + u + 1)
        return carry

    def body(t, carry):
        step(t, t + 1)
        return carry

    n_unrolled = qi // KEY_BLOCK_UNROLL
    lax.fori_loop(0, n_unrolled, unrolled_body, 0)
    lax.fori_loop(n_unrolled * KEY_BLOCK_UNROLL, qi, body, 0)
    step(qi, None)

    lam = _lam(lq1_ref, lk1_ref, lq2_ref, lk2_ref, lam_init)
    o1 = acc_scr[0, 0:V_HEAD, :] / acc_scr[0, V_HEAD:V_HEAD + 1, :]
    o2 = acc_scr[1, 0:V_HEAD, :] / acc_scr[1, V_HEAD:V_HEAD + 1, :]
    o = o1 - lam * o2
    o = o * lax.rsqrt(jnp.mean(o * o, axis=0, keepdims=True) + SUBLN_EPS)
    o_ref[0] = o.T * sw_ref[...] * (1.0 - lam_init)


def _prompt_attention(q, kb, vt, lams, subln_w, *, tq, lam_init):
    b, _, s, _ = q.shape
    assert s % tq == 0 and tq % CHUNK == 0
    lam_spec = pl.BlockSpec((1, D_HEAD), lambda bi, h, i: (0, 0))
    vmem = 2 * 2 * s * V_HEAD * 2 + (2 + 2) * tq * tq * 4 + 8 * tq * tq * 4 + 8 * tq * ACC_ROWS * 4
    return pl.pallas_call(
        functools.partial(_attn_kernel, tq=tq, lam_init=lam_init),
        grid=(b, N_HEADS, s // tq),
        in_specs=[
            pl.BlockSpec(memory_space=pltpu.SMEM),
            pl.BlockSpec((1, 1, tq, V_HEAD), lambda bi, h, i: (bi, h, i, 0)),
            pl.BlockSpec((1, 1, s, V_HEAD), lambda bi, h, i: (bi, h, 0, 0)),
            pl.BlockSpec((1, 1, V_HEAD, s), lambda bi, h, i: (bi, h, 0, 0)),
            lam_spec, lam_spec, lam_spec, lam_spec,
            pl.BlockSpec((1, V_HEAD), lambda bi, h, i: (0, 0)),
        ],
        out_specs=pl.BlockSpec((1, tq, V_HEAD), lambda bi, h, i: (bi, i, h)),
        out_shape=jax.ShapeDtypeStruct((b, s, ATTN_WIDTH), F32),
        scratch_shapes=[
            pltpu.VMEM((2, V_HEAD, tq), BF16),
            pltpu.VMEM((2, tq, tq), F32),
            pltpu.VMEM((2, tq, tq), F32),
            pltpu.VMEM((2, 1, tq), F32),
            pltpu.VMEM((2, 1, tq), F32),
            pltpu.VMEM((2, ACC_ROWS, tq), F32),
        ],
        compiler_params=pltpu.CompilerParams(
            dimension_semantics=("arbitrary", "arbitrary", "arbitrary"),
            vmem_limit_bytes=vmem + 8 * MIB,
        ),
        name="attn",
    )(_alibi_sigmas(), q, kb, vt, *lams, subln_w.reshape(1, V_HEAD))


def _decode_kernel(sig_ref, q_ref, kn_ref, vn_ref, kc_ref, vc_ref, lq1_ref, lk1_ref, lq2_ref, lk2_ref, sw_ref,
                   o_ref, *, t, past, lam_init):
    hg = pl.program_id(1)
    lam = _lam(lq1_ref, lk1_ref, lq2_ref, lk2_ref, lam_init)
    qpos_c = past + lax.broadcasted_iota(jnp.int32, (t, past), 0)
    kpos_c = lax.broadcasted_iota(jnp.int32, (t, past), 1)
    qpos_n = past + lax.broadcasted_iota(jnp.int32, (t, t), 0)
    kpos_n = past + lax.broadcasted_iota(jnp.int32, (t, t), 1)

    def bias(qpos, kpos, sigma):
        allowed = lax.shift_right_logical(kpos, CHUNK_SHIFT) <= lax.shift_right_logical(qpos, CHUNK_SHIFT)
        return jnp.where(allowed, -sigma * jnp.abs(qpos - kpos).astype(F32), MASK_VALUE)

    lane = lax.broadcasted_iota(jnp.int32, (t, V_HEAD), 1)
    nt = (((1,), (1,)), ((), ()))
    for hh in range(HEADS_PER_TILE):
        sigma = sig_ref[hg * HEADS_PER_TILE + hh]
        cols = slice(hh * V_HEAD, (hh + 1) * V_HEAD)
        q = q_ref[0, hh]
        kc = kc_ref[0, hh].astype(BF16)
        vc = vc_ref[0, hh].astype(BF16)
        kn = kn_ref[0, hh].astype(BF16)
        vn = vn_ref[0, hh].astype(BF16)
        bias_c = bias(qpos_c, kpos_c, sigma)
        bias_n = bias(qpos_n, kpos_n, sigma)
        outs = []
        for c in range(2):
            keep = (lane < D_HEAD) if c == 0 else (lane >= D_HEAD)
            qc = jnp.where(keep, q, jnp.zeros_like(q))
            s_c = lax.dot_general(qc, kc, nt, preferred_element_type=F32) + bias_c
            s_n = lax.dot_general(qc, kn, nt, preferred_element_type=F32) + bias_n
            m = jnp.maximum(jnp.max(s_c, axis=-1, keepdims=True), jnp.max(s_n, axis=-1, keepdims=True))
            p_c = jnp.exp2(s_c - m)
            p_n = jnp.exp2(s_n - m)
            l = jnp.sum(p_c, axis=-1, keepdims=True) + jnp.sum(p_n, axis=-1, keepdims=True)
            pv = (jnp.dot(p_c.astype(BF16), vc, preferred_element_type=F32)
                  + jnp.dot(p_n.astype(BF16), vn, preferred_element_type=F32))
            outs.append(pv / l)
        o = outs[0] - lam * outs[1]
        o = o * lax.rsqrt(jnp.mean(o * o, axis=-1, keepdims=True) + SUBLN_EPS)
        o_ref[0, :, cols] = o * sw_ref[...] * (1.0 - lam_init)


def _decode_attention(q, k_new, v_new, cache_k, cache_v, lams, subln_w, *, db, lam_init):
    rows = q.shape[2]
    t = rows // db
    past = cache_k.shape[2]
    new_spec = pl.BlockSpec((1, HEADS_PER_TILE, t, V_HEAD), lambda bi, g: (0, g, bi, 0))
    cache_spec = pl.BlockSpec((1, HEADS_PER_TILE, past, V_HEAD), lambda bi, g: (bi, g, 0, 0))
    lam_spec = pl.BlockSpec((1, D_HEAD), lambda bi, g: (0, 0))
    vmem = 2 * 2 * past * COL_TILE * 4 + 8 * past * V_HEAD * 4 + 16 * t * past * 4
    return pl.pallas_call(
        functools.partial(_decode_kernel, t=t, past=past, lam_init=lam_init),
        grid=(db, N_HEADS // HEADS_PER_TILE),
        in_specs=[
            pl.BlockSpec(memory_space=pltpu.SMEM),
            new_spec, new_spec, new_spec, cache_spec, cache_spec,
            lam_spec, lam_spec, lam_spec, lam_spec,
            pl.BlockSpec((1, V_HEAD), lambda bi, g: (0, 0)),
        ],
        out_specs=pl.BlockSpec((1, t, COL_TILE), lambda bi, g: (bi, 0, g)),
        out_shape=jax.ShapeDtypeStruct((db, t, ATTN_WIDTH), F32),
        compiler_params=pltpu.CompilerParams(
            dimension_semantics=("arbitrary", "arbitrary"),
            vmem_limit_bytes=vmem + 8 * MIB,
        ),
        name="decode_attn",
    )(_alibi_sigmas(), q, k_new, v_new, cache_k, cache_v, *lams, subln_w.reshape(1, V_HEAD))


def _out_kernel(pool_ref, gp_ref, attn_ref, ga_ref, x_ref, wo_ref, fw_ref, y_ref):
    mix_p = (jax.nn.silu(gp_ref[0]) * pool_ref[0]).astype(BF16)
    mix_a = (jax.nn.silu(ga_ref[0]) * attn_ref[0]).astype(BF16)
    r = jnp.dot(mix_p, wo_ref[0:POOL_WIDTH, :], preferred_element_type=F32)
    r = r + jnp.dot(mix_a, wo_ref[POOL_WIDTH:, :], preferred_element_type=F32)
    hres = x_ref[0] + r
    ms = jnp.mean(hres * hres, axis=-1, keepdims=True)
    y_ref[0] = hres * lax.rsqrt(ms + NORM_EPS) * fw_ref[...]


def _merge_out(pool_out, g_pool, attn_out, g_attn, x, w_out_bf16, final_norm_w, *, tm):
    b, s, d = x.shape
    assert s % tm == 0
    row = lambda width: pl.BlockSpec((1, tm, width), lambda bi, i: (bi, i, 0))
    vmem = 2 * tm * (2 * POOL_WIDTH + 2 * ATTN_WIDTH + 2 * d) * 4 + 2 * d * d * 2 + 6 * tm * d * 4
    return pl.pallas_call(
        _out_kernel,
        grid=(b, s // tm),
        in_specs=[
            row(POOL_WIDTH), row(POOL_WIDTH), row(ATTN_WIDTH), row(ATTN_WIDTH), row(d),
            pl.BlockSpec((d, d), lambda bi, i: (0, 0)),
            pl.BlockSpec((1, d), lambda bi, i: (0, 0)),
        ],
        out_specs=row(d),
        out_shape=jax.ShapeDtypeStruct((b, s, d), F32),
        compiler_params=pltpu.CompilerParams(
            dimension_semantics=("arbitrary", "arbitrary"),
            vmem_limit_bytes=vmem + 8 * MIB,
        ),
        name="merge_out",
    )(pool_out, g_pool, attn_out, g_attn, x, w_out_bf16, final_norm_w.reshape(1, d))


def _tile(n, target):
    t = min(n, target)
    assert n % t == 0
    return t


def _last_rows(hist, u):
    if u.shape[1] >= POOL_HIST:
        return u[:, u.shape[1] - POOL_HIST:]
    return jnp.concatenate([hist, u], axis=1)[:, -POOL_HIST:]


def kernel(x_prompt, x_sample, cache_k, cache_v, state_pool, norm_w, w_in, w_pool, pool_scale,
           lambda_q1, lambda_k1, lambda_q2, lambda_k2, subln_w, w_out, final_norm_w):
    depth = norm_w.shape[0]
    assert depth == 1, "the final norm is fused into the single layer's output kernel"
    layer = 0
    lam_init = _lam_init(layer)
    b, s, d = x_prompt.shape
    db, t, _ = x_sample.shape

    w_in_b = w_in[layer].astype(BF16)
    w_out_b = w_out[layer].astype(BF16)
    w_pool_b = w_pool[layer].astype(BF16)
    lams = tuple(p[layer].reshape(1, D_HEAD) for p in (lambda_q1, lambda_k1, lambda_q2, lambda_k2))
    position_major = lambda a: jnp.transpose(a, (0, 2, 1, 3))

    u, gp, q, k, v, ga, kb, vt = _project(x_prompt, norm_w[layer], w_in_b, tm=_tile(s, 512), attn_layouts=True)
    hist0 = jnp.zeros((b, HALO, POOL_WIDTH), F32)
    pool_out = _pool_mix(u, hist0, w_pool_b, pool_scale[layer], tm=_tile(s, 512), hist_valid=0)
    attn_out = _prompt_attention(q, kb, vt, lams, subln_w[layer], tq=_tile(s, 512), lam_init=lam_init)
    y_prompt = _merge_out(pool_out, gp, attn_out, ga, x_prompt, w_out_b, final_norm_w, tm=_tile(s, 256))

    xs = x_sample.reshape(1, db * t, d)
    u2, gp2, q2, k2, v2, ga2 = _project(xs, norm_w[layer], w_in_b, tm=_tile(db * t, 512), attn_layouts=False)
    hist = jnp.concatenate([jnp.zeros((db, HALO - POOL_HIST, POOL_WIDTH), F32), state_pool[layer]], axis=1)
    u2s = u2.reshape(db, t, POOL_WIDTH)
    pool_out2 = _pool_mix(u2s, hist, w_pool_b, pool_scale[layer], tm=t, hist_valid=POOL_HIST)
    attn2 = _decode_attention(q2, k2, v2, position_major(cache_k[layer]), position_major(cache_v[layer]),
                              lams, subln_w[layer], db=db, lam_init=lam_init)
    flat = lambda a: a.reshape(1, db * t, a.shape[-1])
    y_sample = _merge_out(flat(pool_out2), gp2, flat(attn2), ga2, xs, w_out_b, final_norm_w,
                          tm=_tile(db * t, 256)).reshape(db, t, d)

    per_stream = lambda a: position_major(a).reshape(db, t, N_HEADS, V_HEAD)[None]
    return (y_prompt, y_sample, position_major(k)[None], position_major(v)[None], _last_rows(hist0[:, 1:], u)[None],
            per_stream(k2), per_stream(v2), _last_rows(state_pool[layer], u2s)[None])
```

```python
import functools
import math

import jax
import jax.numpy as jnp
import numpy as np
from jax import lax
from jax.experimental import pallas as pl
from jax.experimental.pallas import tpu as pltpu

F32 = jnp.float32
BF16 = jnp.bfloat16

D_MODEL = 2048
CHUNK = 64
CHUNK_SHIFT = 6
POOL_WINDOWS = (2, 4, 8, 16)
POOL_GROUP_DIM = 128
POOL_WIDTH = len(POOL_WINDOWS) * POOL_GROUP_DIM
POOL_HIST = max(POOL_WINDOWS) - 1
HALO = POOL_HIST + 1
ATTN_WIDTH = D_MODEL - POOL_WIDTH
N_HEADS = 12
V_HEAD = ATTN_WIDTH // N_HEADS
D_HEAD = V_HEAD // 2
IN_WIDTH = 2 * POOL_WIDTH + 4 * ATTN_WIDTH
NORM_EPS = 1e-6
SUBLN_EPS = 1e-5
MASK_VALUE = -1e30
LOG2E = math.log2(math.e)
Q_SCALE = (D_HEAD ** -0.5) * LOG2E

COL_TILE = 512
N_COL_TILES = IN_WIDTH // COL_TILE
ATTN_COL_TILES = ATTN_WIDTH // COL_TILE
HEADS_PER_TILE = COL_TILE // V_HEAD
SEG_GP, SEG_Q, SEG_K, SEG_V, SEG_GA = 1, 2, 2 + ATTN_COL_TILES, 2 + 2 * ATTN_COL_TILES, 2 + 3 * ATTN_COL_TILES

KEY_BLOCK = 512
N_POS_FEATURES = 9
BF16_SUBLANES = 16
MIB = 1024 * 1024


def _alibi_sigmas():
    def pow2_slopes(m):
        start = 2.0 ** (-8.0 / m)
        return [start ** (i + 1) for i in range(m)]

    n = N_HEADS
    if math.log2(n).is_integer():
        s = pow2_slopes(n)
    else:
        c = 2 ** int(math.floor(math.log2(n)))
        s = pow2_slopes(c) + pow2_slopes(2 * c)[0::2][: n - c]
    return jnp.asarray(np.array(s, dtype=np.float32) * np.float32(LOG2E))


def _lam_init(layer):
    return 0.8 - 0.6 * math.exp(-0.3 * layer)


def _lam(lq1_ref, lk1_ref, lq2_ref, lk2_ref, lam_init):
    a = jnp.sum(lq1_ref[...] * lk1_ref[...], axis=-1, keepdims=True)
    b = jnp.sum(lq2_ref[...] * lk2_ref[...], axis=-1, keepdims=True)
    return jnp.exp(a) - jnp.exp(b) + lam_init


def _bf16_parts(x):
    p1 = x.astype(BF16).astype(F32)
    p2 = (x - p1).astype(BF16).astype(F32)
    p3 = (x - p1 - p2).astype(BF16).astype(F32)
    return (p1, p2, p3)


def _position_features(kj, slot):
    hi = lax.shift_left(lax.shift_right_logical(kj, 4), 4).astype(F32)
    lo = (kj & 15).astype(F32)
    return jnp.where(slot < 0, 0.0, jnp.where(slot < 3, hi, jnp.where(slot < 6, lo, jnp.where(slot < 9, 1.0, 0.0))))


def _proj_kernel(x_ref, nw_ref, w_ref, u_ref, gp_ref, q_ref, k_ref, v_ref, ga_ref, *rest, tm, attn_layouts):
    if attn_layouts:
        ka_ref, kb_ref, vt_ref, h_scr = rest
    else:
        (h_scr,) = rest
    i = pl.program_id(1)
    j = pl.program_id(2)

    @pl.when(j == 0)
    def _():
        x = x_ref[0]
        ms = jnp.mean(x * x, axis=-1, keepdims=True)
        h_scr[...] = (x * lax.rsqrt(ms + NORM_EPS) * nw_ref[...]).astype(BF16)

    def tile():
        return jnp.dot(h_scr[...], w_ref[...], preferred_element_type=F32)

    @pl.when(j == 0)
    def _():
        u_ref[0] = tile()

    @pl.when(j == SEG_GP)
    def _():
        gp_ref[0] = tile()

    @pl.when((j >= SEG_Q) & (j < SEG_K))
    def _():
        z = tile()
        for hh in range(HEADS_PER_TILE):
            q_ref[0, hh] = (z[:, hh * V_HEAD:(hh + 1) * V_HEAD] * Q_SCALE).astype(BF16)

    @pl.when((j >= SEG_K) & (j < SEG_V))
    def _():
        z = tile()
        if attn_layouts:
            lane = lax.broadcasted_iota(jnp.int32, (tm, V_HEAD), 1)
            kj = (i * tm + lax.broadcasted_iota(jnp.int32, (tm, V_HEAD), 0)) & (KEY_BLOCK - 1)
            feat_a = _position_features(kj, lane - D_HEAD)
            feat_b = _position_features(kj, lane)
        for hh in range(HEADS_PER_TILE):
            zh = z[:, hh * V_HEAD:(hh + 1) * V_HEAD]
            k_ref[0, hh] = zh
            if attn_layouts:
                ka_ref[0, hh] = jnp.where(lane < D_HEAD, zh, feat_a).astype(BF16)
                kb_ref[0, hh] = jnp.where(lane >= D_HEAD, zh, feat_b).astype(BF16)

    @pl.when((j >= SEG_V) & (j < SEG_GA))
    def _():
        z = tile()
        for hh in range(HEADS_PER_TILE):
            zh = z[:, hh * V_HEAD:(hh + 1) * V_HEAD]
            v_ref[0, hh] = zh
            if attn_layouts:
                vt_ref[0, hh] = zh.T.astype(BF16)

    @pl.when(j >= SEG_GA)
    def _():
        ga_ref[0] = tile()


def _project(x, norm_w, w_in_bf16, *, tm, attn_layouts):
    b, s, d = x.shape
    assert s % tm == 0 and d == D_MODEL
    grid = (b, s // tm, N_COL_TILES)

    def seg(first, j):
        return jnp.clip(j - first, 0, ATTN_COL_TILES - 1)

    row_tile = lambda bi, i, j: (bi, i, 0)
    head_block = (1, HEADS_PER_TILE, tm, V_HEAD)
    head_map = lambda first: (lambda bi, i, j: (bi, seg(first, j), i, 0))
    out_shape = [
        jax.ShapeDtypeStruct((b, s, POOL_WIDTH), F32),
        jax.ShapeDtypeStruct((b, s, POOL_WIDTH), F32),
        jax.ShapeDtypeStruct((b, N_HEADS, s, V_HEAD), BF16),
        jax.ShapeDtypeStruct((b, N_HEADS, s, V_HEAD), F32),
        jax.ShapeDtypeStruct((b, N_HEADS, s, V_HEAD), F32),
        jax.ShapeDtypeStruct((b, s, ATTN_WIDTH), F32),
    ]
    out_specs = [
        pl.BlockSpec((1, tm, COL_TILE), row_tile),
        pl.BlockSpec((1, tm, COL_TILE), row_tile),
        pl.BlockSpec(head_block, head_map(SEG_Q)),
        pl.BlockSpec(head_block, head_map(SEG_K)),
        pl.BlockSpec(head_block, head_map(SEG_V)),
        pl.BlockSpec((1, tm, COL_TILE), lambda bi, i, j: (bi, i, seg(SEG_GA, j))),
    ]
    n_f32_tiles, n_bf16_tiles = 5, 1
    if attn_layouts:
        out_shape += [
            jax.ShapeDtypeStruct((b, N_HEADS, s, V_HEAD), BF16),
            jax.ShapeDtypeStruct((b, N_HEADS, s, V_HEAD), BF16),
            jax.ShapeDtypeStruct((b, N_HEADS, V_HEAD, s), BF16),
        ]
        out_specs += [
            pl.BlockSpec(head_block, head_map(SEG_K)),
            pl.BlockSpec(head_block, head_map(SEG_K)),
            pl.BlockSpec((1, HEADS_PER_TILE, V_HEAD, tm), lambda bi, i, j: (bi, seg(SEG_V, j), 0, i)),
        ]
        n_bf16_tiles += 3
    vmem = (tm * d * 4 + tm * d * 2 + 2 * d * COL_TILE * 2
            + 2 * tm * COL_TILE * (n_f32_tiles * 4 + n_bf16_tiles * 2) + 3 * tm * COL_TILE * 4)
    return pl.pallas_call(
        functools.partial(_proj_kernel, tm=tm, attn_layouts=attn_layouts),
        grid=grid,
        in_specs=[
            pl.BlockSpec((1, tm, d), row_tile, pipeline_mode=pl.Buffered(1)),
            pl.BlockSpec((1, d), lambda bi, i, j: (0, 0)),
            pl.BlockSpec((d, COL_TILE), lambda bi, i, j: (0, j)),
        ],
        out_specs=out_specs,
        out_shape=out_shape,
        scratch_shapes=[pltpu.VMEM((tm, d), BF16)],
        compiler_params=pltpu.CompilerParams(
            dimension_semantics=("arbitrary", "arbitrary", "arbitrary"),
            vmem_limit_bytes=vmem + 4 * MIB,
        ),
        name="proj",
    )(x, norm_w.reshape(1, d), w_in_bf16)


def _pool_kernel(u_ref, prev_ref, hist_ref, wp_ref, ps_ref, o_ref, ext_scr, *, tm, hist_valid):
    i = pl.program_id(1)
    ext_scr[0:HALO, :] = jnp.where(i == 0, hist_ref[0], prev_ref[0])
    ext_scr[HALO:HALO + tm, :] = u_ref[0]
    frame = i * tm + lax.broadcasted_iota(jnp.int32, (tm, POOL_GROUP_DIM), 0)
    for g, w in enumerate(POOL_WINDOWS):
        lanes = slice(g * POOL_GROUP_DIM, (g + 1) * POOL_GROUP_DIM)
        cur = ext_scr[HALO:HALO + tm, lanes]
        acc = cur
        for dlt in range(1, w):
            acc = acc + ext_scr[HALO - dlt:HALO - dlt + tm, lanes]
        count = jnp.minimum(w, hist_valid + frame + 1).astype(F32)
        diff = acc / count - cur
        y = jnp.dot(diff.astype(BF16), wp_ref[g], preferred_element_type=F32)
        o_ref[0, :, lanes] = y * ps_ref[:, lanes]


def _pool_mix(u, hist, w_pool_bf16, pool_scale, *, tm, hist_valid):
    b, t, c = u.shape
    assert t % tm == 0 and tm % HALO == 0
    blocks_per_tile = tm // HALO
    return pl.pallas_call(
        functools.partial(_pool_kernel, tm=tm, hist_valid=hist_valid),
        grid=(b, t // tm),
        in_specs=[
            pl.BlockSpec((1, tm, c), lambda bi, i: (bi, i, 0)),
            pl.BlockSpec((1, HALO, c), lambda bi, i: (bi, jnp.maximum(i * blocks_per_tile - 1, 0), 0)),
            pl.BlockSpec((1, HALO, c), lambda bi, i: (bi, 0, 0)),
            pl.BlockSpec((len(POOL_WINDOWS), POOL_GROUP_DIM, POOL_GROUP_DIM), lambda bi, i: (0, 0, 0)),
            pl.BlockSpec((1, c), lambda bi, i: (0, 0)),
        ],
        out_specs=pl.BlockSpec((1, tm, c), lambda bi, i: (bi, i, 0)),
        out_shape=jax.ShapeDtypeStruct((b, t, c), F32),
        scratch_shapes=[pltpu.VMEM((HALO + tm, c), F32)],
        compiler_params=pltpu.CompilerParams(dimension_semantics=("arbitrary", "arbitrary")),
        name="pool",
    )(u, u, hist, w_pool_bf16, pool_scale.reshape(1, c))


ACC_ROWS = V_HEAD + BF16_SUBLANES
KEY_BLOCK_UNROLL = 4


def _attn_kernel(sig_ref, q_ref, ka_ref, kb_ref, vt_ref, ga_ref, lq1_ref, lk1_ref, lq2_ref, lk2_ref, sw_ref, o_ref,
                 qp_scr, qfeat_scr, bias_scr, sh_scr, mb_scr, m_scr, acc_scr, *, lam_init):
    tq = KEY_BLOCK
    h = pl.program_id(1)
    qi = pl.program_id(2)
    sigma = sig_ref[h]
    decay = sigma * tq
    k_refs = (ka_ref, kb_ref)

    @pl.when(qi == 0)
    def _():
        kj = lax.broadcasted_iota(jnp.int32, (tq, tq), 0)
        qq = lax.broadcasted_iota(jnp.int32, (tq, tq), 1)
        rel = (qq - kj).astype(F32)
        allowed = lax.shift_right_logical(kj, CHUNK_SHIFT) <= lax.shift_right_logical(qq, CHUNK_SHIFT)
        bias_scr[...] = jnp.where(allowed, jnp.where(rel >= 0.0, 0.0, 2.0 * sigma * rel), MASK_VALUE)
        row = lax.broadcasted_iota(jnp.int32, (V_HEAD, tq), 0)
        col = lax.broadcasted_iota(jnp.int32, (V_HEAD, tq), 1).astype(F32)
        parts = _bf16_parts(jnp.full((V_HEAD, tq), sigma, F32)) * 2 + _bf16_parts(-sigma * col)
        assert len(parts) == N_POS_FEATURES
        for c, slot in enumerate((row - D_HEAD, row)):
            feat = jnp.zeros((V_HEAD, tq), F32)
            for n, part in enumerate(parts):
                feat = jnp.where(slot == n, part, feat)
            qfeat_scr[c] = feat

    qt = q_ref[0, 0].astype(F32).T
    row = lax.broadcasted_iota(jnp.int32, (V_HEAD, tq), 0)
    qp_scr[0] = jnp.where(row < D_HEAD, qt, qfeat_scr[0]).astype(BF16)
    qp_scr[1] = jnp.where(row >= D_HEAD, qt, qfeat_scr[1]).astype(BF16)
    m_scr[...] = jnp.full(m_scr.shape, MASK_VALUE, F32)
    acc_scr[...] = jnp.zeros(acc_scr.shape, F32)
    ones_rows = jnp.ones((BF16_SUBLANES, tq), BF16)

    def step(value_kb, score_kb, diagonal=False):
        if score_kb is not None:
            k0 = pl.multiple_of(score_kb * tq, tq)
        if value_kb is not None:
            v0 = pl.multiple_of(value_kb * tq, tq)
            vblk = jnp.concatenate([vt_ref[0, 0, :, pl.ds(v0, tq)], ones_rows], axis=0)
        for c in range(2):
            if value_kb is not None:
                m_old = m_scr[c] - decay
                m_new = jnp.maximum(m_old, mb_scr[c])
                p = jnp.exp2(sh_scr[c] - m_new).astype(BF16)
                alpha = jnp.exp2(m_old - m_new)
            if score_kb is not None:
                kblk = k_refs[c][0, 0, pl.ds(k0, tq), :]
                sh = jnp.dot(kblk, qp_scr[c], preferred_element_type=F32)
                if diagonal:
                    sh = sh + bias_scr[...]
                sh_scr[c] = sh
                mb_scr[c] = jnp.max(sh, axis=0, keepdims=True)
            if value_kb is not None:
                acc_scr[c] = alpha * acc_scr[c] + jnp.dot(vblk, p, preferred_element_type=F32)
                m_scr[c] = m_new

    @pl.when(qi == 0)
    def _():
        step(None, 0, diagonal=True)

    @pl.when(qi > 0)
    def _():
        step(None, 0)

    def unrolled_body(i, carry):
        for u in range(KEY_BLOCK_UNROLL):
            step(i * KEY_BLOCK_UNROLL + u, i * KEY_BLOCK_UNROLL + u + 1)
        return carry

    def body(t, carry):
        step(t, t + 1)
        return carry

    n_plain = jnp.maximum(qi - 1, 0)
    n_unrolled = n_plain // KEY_BLOCK_UNROLL
    lax.fori_loop(0, n_unrolled, unrolled_body, 0)
    lax.fori_loop(n_unrolled * KEY_BLOCK_UNROLL, n_plain, body, 0)

    @pl.when(qi > 0)
    def _():
        step(qi - 1, qi, diagonal=True)

    step(qi, None)

    lam = _lam(lq1_ref, lk1_ref, lq2_ref, lk2_ref, lam_init)
    o1 = acc_scr[0, 0:V_HEAD, :] / acc_scr[0, V_HEAD:V_HEAD + 1, :]
    o2 = acc_scr[1, 0:V_HEAD, :] / acc_scr[1, V_HEAD:V_HEAD + 1, :]
    o = o1 - lam * o2
    o = o * lax.rsqrt(jnp.mean(o * o, axis=0, keepdims=True) + SUBLN_EPS)
    attn = o.T * sw_ref[...] * (1.0 - lam_init)
    o_ref[0] = (jax.nn.silu(ga_ref[0]) * attn).astype(BF16)


def _prompt_attention(q, ka, kb, vt, g_attn, lams, subln_w, *, lam_init):
    b, _, s, _ = q.shape
    tq = KEY_BLOCK
    assert s % tq == 0 and tq % CHUNK == 0
    lam_spec = pl.BlockSpec((1, D_HEAD), lambda bi, h, i: (0, 0))
    whole_seq = lambda bi, h, i: (bi, h, 0, 0)
    vmem = 2 * 3 * s * V_HEAD * 2 + (1 + 2) * tq * tq * 4 + 8 * tq * tq * 4 + 12 * tq * ACC_ROWS * 4
    return pl.pallas_call(
        functools.partial(_attn_kernel, lam_init=lam_init),
        grid=(b, N_HEADS, s // tq),
        in_specs=[
            pl.BlockSpec(memory_space=pltpu.SMEM),
            pl.BlockSpec((1, 1, tq, V_HEAD), lambda bi, h, i: (bi, h, i, 0)),
            pl.BlockSpec((1, 1, s, V_HEAD), whole_seq),
            pl.BlockSpec((1, 1, s, V_HEAD), whole_seq),
            pl.BlockSpec((1, 1, V_HEAD, s), whole_seq),
            pl.BlockSpec((1, tq, V_HEAD), lambda bi, h, i: (bi, i, h)),
            lam_spec, lam_spec, lam_spec, lam_spec,
            pl.BlockSpec((1, V_HEAD), lambda bi, h, i: (0, 0)),
        ],
        out_specs=pl.BlockSpec((1, tq, V_HEAD), lambda bi, h, i: (bi, i, h)),
        out_shape=jax.ShapeDtypeStruct((b, s, ATTN_WIDTH), BF16),
        scratch_shapes=[
            pltpu.VMEM((2, V_HEAD, tq), BF16),
            pltpu.VMEM((2, V_HEAD, tq), F32),
            pltpu.VMEM((tq, tq), F32),
            pltpu.VMEM((2, tq, tq), F32),
            pltpu.VMEM((2, 1, tq), F32),
            pltpu.VMEM((2, 1, tq), F32),
            pltpu.VMEM((2, ACC_ROWS, tq), F32),
        ],
        compiler_params=pltpu.CompilerParams(
            dimension_semantics=("arbitrary", "arbitrary", "arbitrary"),
            vmem_limit_bytes=vmem + 8 * MIB,
        ),
        name="attn",
    )(_alibi_sigmas(), q, ka, kb, vt, g_attn, *lams, subln_w.reshape(1, V_HEAD))


def _decode_kernel(sig_ref, q_ref, kn_ref, vn_ref, kc_ref, vc_ref, ga_ref, lq1_ref, lk1_ref, lq2_ref, lk2_ref,
                   sw_ref, o_ref, *, t, past, lam_init):
    hg = pl.program_id(1)
    lam = _lam(lq1_ref, lk1_ref, lq2_ref, lk2_ref, lam_init)
    qpos_c = past + lax.broadcasted_iota(jnp.int32, (t, past), 0)
    kpos_c = lax.broadcasted_iota(jnp.int32, (t, past), 1)
    qpos_n = past + lax.broadcasted_iota(jnp.int32, (t, t), 0)
    kpos_n = past + lax.broadcasted_iota(jnp.int32, (t, t), 1)

    def bias(qpos, kpos, sigma):
        allowed = lax.shift_right_logical(kpos, CHUNK_SHIFT) <= lax.shift_right_logical(qpos, CHUNK_SHIFT)
        return jnp.where(allowed, -sigma * jnp.abs(qpos - kpos).astype(F32), MASK_VALUE)

    lane = lax.broadcasted_iota(jnp.int32, (t, V_HEAD), 1)
    nt = (((1,), (1,)), ((), ()))
    for hh in range(HEADS_PER_TILE):
        sigma = sig_ref[hg * HEADS_PER_TILE + hh]
        cols = slice(hh * V_HEAD, (hh + 1) * V_HEAD)
        q = q_ref[0, hh]
        kc = kc_ref[0, hh].astype(BF16)
        vc = vc_ref[0, hh].astype(BF16)
        kn = kn_ref[0, hh].astype(BF16)
        vn = vn_ref[0, hh].astype(BF16)
        bias_c = bias(qpos_c, kpos_c, sigma)
        bias_n = bias(qpos_n, kpos_n, sigma)
        outs = []
        for c in range(2):
            keep = (lane < D_HEAD) if c == 0 else (lane >= D_HEAD)
            qc = jnp.where(keep, q, jnp.zeros_like(q))
            s_c = lax.dot_general(qc, kc, nt, preferred_element_type=F32) + bias_c
            s_n = lax.dot_general(qc, kn, nt, preferred_element_type=F32) + bias_n
            m = jnp.maximum(jnp.max(s_c, axis=-1, keepdims=True), jnp.max(s_n, axis=-1, keepdims=True))
            p_c = jnp.exp2(s_c - m)
            p_n = jnp.exp2(s_n - m)
            l = jnp.sum(p_c, axis=-1, keepdims=True) + jnp.sum(p_n, axis=-1, keepdims=True)
            pv = (jnp.dot(p_c.astype(BF16), vc, preferred_element_type=F32)
                  + jnp.dot(p_n.astype(BF16), vn, preferred_element_type=F32))
            outs.append(pv / l)
        o = outs[0] - lam * outs[1]
        o = o * lax.rsqrt(jnp.mean(o * o, axis=-1, keepdims=True) + SUBLN_EPS)
        attn = o * sw_ref[...] * (1.0 - lam_init)
        o_ref[0, :, cols] = (jax.nn.silu(ga_ref[0, :, cols]) * attn).astype(BF16)


def _decode_attention(q, k_new, v_new, cache_k, cache_v, g_attn, lams, subln_w, *, db, lam_init):
    rows = q.shape[2]
    t = rows // db
    past = cache_k.shape[2]
    new_spec = pl.BlockSpec((1, HEADS_PER_TILE, t, V_HEAD), lambda bi, g: (0, g, bi, 0))
    cache_spec = pl.BlockSpec((1, HEADS_PER_TILE, past, V_HEAD), lambda bi, g: (bi, g, 0, 0))
    row_spec = pl.BlockSpec((1, t, COL_TILE), lambda bi, g: (0, bi, g))
    lam_spec = pl.BlockSpec((1, D_HEAD), lambda bi, g: (0, 0))
    vmem = 2 * 2 * past * COL_TILE * 4 + 8 * past * V_HEAD * 4 + 16 * t * past * 4
    return pl.pallas_call(
        functools.partial(_decode_kernel, t=t, past=past, lam_init=lam_init),
        grid=(db, N_HEADS // HEADS_PER_TILE),
        in_specs=[
            pl.BlockSpec(memory_space=pltpu.SMEM),
            new_spec, new_spec, new_spec, cache_spec, cache_spec, row_spec,
            lam_spec, lam_spec, lam_spec, lam_spec,
            pl.BlockSpec((1, V_HEAD), lambda bi, g: (0, 0)),
        ],
        out_specs=row_spec,
        out_shape=jax.ShapeDtypeStruct((1, rows, ATTN_WIDTH), BF16),
        compiler_params=pltpu.CompilerParams(
            dimension_semantics=("arbitrary", "arbitrary"),
            vmem_limit_bytes=vmem + 8 * MIB,
        ),
        name="decode_attn",
    )(_alibi_sigmas(), q, k_new, v_new, cache_k, cache_v, g_attn, *lams, subln_w.reshape(1, V_HEAD))


def _out_kernel(pool_ref, gp_ref, mix_a_ref, x_ref, wo_ref, fw_ref, y_ref):
    mix_p = (jax.nn.silu(gp_ref[0]) * pool_ref[0]).astype(BF16)
    r = jnp.dot(mix_p, wo_ref[0:POOL_WIDTH, :], preferred_element_type=F32)
    r = r + jnp.dot(mix_a_ref[0], wo_ref[POOL_WIDTH:, :], preferred_element_type=F32)
    hres = x_ref[0] + r
    ms = jnp.mean(hres * hres, axis=-1, keepdims=True)
    y_ref[0] = hres * lax.rsqrt(ms + NORM_EPS) * fw_ref[...]


def _merge_out(pool_out, g_pool, mix_attn, x, w_out_bf16, final_norm_w, *, tm):
    b, s, d = x.shape
    assert s % tm == 0
    row = lambda width: pl.BlockSpec((1, tm, width), lambda bi, i: (bi, i, 0))
    vmem = 2 * tm * ((2 * POOL_WIDTH + 2 * d) * 4 + ATTN_WIDTH * 2) + 2 * d * d * 2 + 6 * tm * d * 4
    return pl.pallas_call(
        _out_kernel,
        grid=(b, s // tm),
        in_specs=[
            row(POOL_WIDTH), row(POOL_WIDTH), row(ATTN_WIDTH), row(d),
            pl.BlockSpec((d, d), lambda bi, i: (0, 0)),
            pl.BlockSpec((1, d), lambda bi, i: (0, 0)),
        ],
        out_specs=row(d),
        out_shape=jax.ShapeDtypeStruct((b, s, d), F32),
        compiler_params=pltpu.CompilerParams(
            dimension_semantics=("arbitrary", "arbitrary"),
            vmem_limit_bytes=vmem + 8 * MIB,
        ),
        name="merge_out",
    )(pool_out, g_pool, mix_attn, x, w_out_bf16, final_norm_w.reshape(1, d))


def _tile(n, target):
    t = min(n, target)
    while n % t:
        t //= 2
    return t


def _last_rows(hist, u):
    if u.shape[1] >= POOL_HIST:
        return u[:, u.shape[1] - POOL_HIST:]
    return jnp.concatenate([hist, u], axis=1)[:, -POOL_HIST:]


def kernel(x_prompt, x_sample, cache_k, cache_v, state_pool, norm_w, w_in, w_pool, pool_scale,
           lambda_q1, lambda_k1, lambda_q2, lambda_k2, subln_w, w_out, final_norm_w):
    depth = norm_w.shape[0]
    assert depth == 1, "the final norm is fused into the single layer's output kernel"
    layer = 0
    lam_init = _lam_init(layer)
    b, s, d = x_prompt.shape
    db, t, _ = x_sample.shape
    assert s % KEY_BLOCK == 0

    w_in_b = w_in[layer].astype(BF16)
    w_out_b = w_out[layer].astype(BF16)
    w_pool_b = w_pool[layer].astype(BF16)
    lams = tuple(p[layer].reshape(1, D_HEAD) for p in (lambda_q1, lambda_k1, lambda_q2, lambda_k2))
    swap_head_axis = lambda a: jnp.transpose(a, (0, 2, 1, 3))

    u, gp, q, k, v, ga, ka, kb, vt = _project(x_prompt, norm_w[layer], w_in_b, tm=_tile(s, 1024), attn_layouts=True)
    hist0 = jnp.zeros((b, HALO, POOL_WIDTH), F32)
    pool_out = _pool_mix(u, hist0, w_pool_b, pool_scale[layer], tm=_tile(s, 512), hist_valid=0)
    mix_attn = _prompt_attention(q, ka, kb, vt, ga, lams, subln_w[layer], lam_init=lam_init)
    y_prompt = _merge_out(pool_out, gp, mix_attn, x_prompt, w_out_b, final_norm_w, tm=_tile(s, 256))

    xs = x_sample.reshape(1, db * t, d)
    u2, gp2, q2, k2, v2, ga2 = _project(xs, norm_w[layer], w_in_b, tm=_tile(db * t, 512), attn_layouts=False)
    hist = jnp.concatenate([jnp.zeros((db, HALO - POOL_HIST, POOL_WIDTH), F32), state_pool[layer]], axis=1)
    u2s = u2.reshape(db, t, POOL_WIDTH)
    pool_out2 = _pool_mix(u2s, hist, w_pool_b, pool_scale[layer], tm=t, hist_valid=POOL_HIST)
    mix_attn2 = _decode_attention(q2, k2, v2, swap_head_axis(cache_k[layer]), swap_head_axis(cache_v[layer]), ga2,
                                  lams, subln_w[layer], db=db, lam_init=lam_init)
    y_sample = _merge_out(pool_out2.reshape(1, db * t, POOL_WIDTH), gp2, mix_attn2, xs, w_out_b, final_norm_w,
                          tm=_tile(db * t, 256)).reshape(db, t, d)

    per_stream = lambda a: swap_head_axis(a).reshape(db, t, N_HEADS, V_HEAD)[None]
    return (y_prompt, y_sample, swap_head_axis(k)[None], swap_head_axis(v)[None], _last_rows(hist0[:, 1:], u)[None],
            per_stream(k2), per_stream(v2), _last_rows(state_pool[layer], u2s)[None])
```

```python
import functools
import math

import jax
import jax.numpy as jnp
import numpy as np
from jax import lax
from jax.experimental import pallas as pl
from jax.experimental.pallas import tpu as pltpu

F32 = jnp.float32
BF16 = jnp.bfloat16

D_MODEL = 2048
CHUNK = 64
CHUNK_SHIFT = 6
POOL_WINDOWS = (2, 4, 8, 16)
POOL_GROUP_DIM = 128
POOL_WIDTH = len(POOL_WINDOWS) * POOL_GROUP_DIM
POOL_HIST = max(POOL_WINDOWS) - 1
HALO = POOL_HIST + 1
ATTN_WIDTH = D_MODEL - POOL_WIDTH
N_HEADS = 12
V_HEAD = ATTN_WIDTH // N_HEADS
D_HEAD = V_HEAD // 2
IN_WIDTH = 2 * POOL_WIDTH + 4 * ATTN_WIDTH
NORM_EPS = 1e-6
SUBLN_EPS = 1e-5
MASK_VALUE = -1e30
LOG2E = math.log2(math.e)
Q_SCALE = (D_HEAD ** -0.5) * LOG2E

COL_TILE = 512
N_COL_TILES = IN_WIDTH // COL_TILE
ATTN_COL_TILES = ATTN_WIDTH // COL_TILE
HEADS_PER_TILE = COL_TILE // V_HEAD
SEG_GP, SEG_Q, SEG_K, SEG_V, SEG_GA = 1, 2, 2 + ATTN_COL_TILES, 2 + 2 * ATTN_COL_TILES, 2 + 3 * ATTN_COL_TILES

KEY_BLOCK = 512
N_POS_FEATURES = 9
BF16_SUBLANES = 16
MIB = 1024 * 1024


def _alibi_sigmas():
    def pow2_slopes(m):
        start = 2.0 ** (-8.0 / m)
        return [start ** (i + 1) for i in range(m)]

    n = N_HEADS
    if math.log2(n).is_integer():
        s = pow2_slopes(n)
    else:
        c = 2 ** int(math.floor(math.log2(n)))
        s = pow2_slopes(c) + pow2_slopes(2 * c)[0::2][: n - c]
    return jnp.asarray(np.array(s, dtype=np.float32) * np.float32(LOG2E))


def _lam_init(layer):
    return 0.8 - 0.6 * math.exp(-0.3 * layer)


def _lam(lq1_ref, lk1_ref, lq2_ref, lk2_ref, lam_init):
    a = jnp.sum(lq1_ref[...] * lk1_ref[...], axis=-1, keepdims=True)
    b = jnp.sum(lq2_ref[...] * lk2_ref[...], axis=-1, keepdims=True)
    return jnp.exp(a) - jnp.exp(b) + lam_init


def _bf16_parts(x):
    p1 = x.astype(BF16).astype(F32)
    p2 = (x - p1).astype(BF16).astype(F32)
    p3 = (x - p1 - p2).astype(BF16).astype(F32)
    return (p1, p2, p3)


def _position_features(kj, slot):
    hi = lax.shift_left(lax.shift_right_logical(kj, 4), 4).astype(F32)
    lo = (kj & 15).astype(F32)
    return jnp.where(slot < 0, 0.0, jnp.where(slot < 3, hi, jnp.where(slot < 6, lo, jnp.where(slot < 9, 1.0, 0.0))))


def _proj_kernel(x_ref, nw_ref, w_ref, u_ref, gp_ref, q_ref, k_ref, v_ref, ga_ref, *rest, tm, attn_layouts):
    if attn_layouts:
        ka_ref, kb_ref, vt_ref, h_scr = rest
    else:
        (h_scr,) = rest
    i = pl.program_id(1)
    j = pl.program_id(2)

    @pl.when(j == 0)
    def _():
        x = x_ref[0]
        ms = jnp.mean(x * x, axis=-1, keepdims=True)
        h_scr[...] = (x * lax.rsqrt(ms + NORM_EPS) * nw_ref[...]).astype(BF16)

    def tile():
        return jnp.dot(h_scr[...], w_ref[...], preferred_element_type=F32)

    @pl.when(j == 0)
    def _():
        u_ref[0] = tile()

    @pl.when(j == SEG_GP)
    def _():
        gp_ref[0] = tile()

    @pl.when((j >= SEG_Q) & (j < SEG_K))
    def _():
        z = tile()
        for hh in range(HEADS_PER_TILE):
            q_ref[0, hh] = (z[:, hh * V_HEAD:(hh + 1) * V_HEAD] * Q_SCALE).astype(BF16)

    @pl.when((j >= SEG_K) & (j < SEG_V))
    def _():
        z = tile()
        if attn_layouts:
            lane = lax.broadcasted_iota(jnp.int32, (tm, V_HEAD), 1)
            kj = (i * tm + lax.broadcasted_iota(jnp.int32, (tm, V_HEAD), 0)) & (KEY_BLOCK - 1)
            feat_a = _position_features(kj, lane - D_HEAD)
            feat_b = _position_features(kj, lane)
        for hh in range(HEADS_PER_TILE):
            zh = z[:, hh * V_HEAD:(hh + 1) * V_HEAD]
            k_ref[0, hh] = zh
            if attn_layouts:
                ka_ref[0, hh] = jnp.where(lane < D_HEAD, zh, feat_a).astype(BF16)
                kb_ref[0, hh] = jnp.where(lane >= D_HEAD, zh, feat_b).astype(BF16)

    @pl.when((j >= SEG_V) & (j < SEG_GA))
    def _():
        z = tile()
        for hh in range(HEADS_PER_TILE):
            zh = z[:, hh * V_HEAD:(hh + 1) * V_HEAD]
            v_ref[0, hh] = zh
            if attn_layouts:
                vt_ref[0, hh] = zh.T.astype(BF16)

    @pl.when(j >= SEG_GA)
    def _():
        ga_ref[0] = tile()


def _project(x, norm_w, w_in_bf16, *, tm, attn_layouts):
    b, s, d = x.shape
    assert s % tm == 0 and d == D_MODEL
    grid = (b, s // tm, N_COL_TILES)

    def seg(first, j):
        return jnp.clip(j - first, 0, ATTN_COL_TILES - 1)

    row_tile = lambda bi, i, j: (bi, i, 0)
    head_block = (1, HEADS_PER_TILE, tm, V_HEAD)
    head_map = lambda first: (lambda bi, i, j: (bi, seg(first, j), i, 0))
    out_shape = [
        jax.ShapeDtypeStruct((b, s, POOL_WIDTH), F32),
        jax.ShapeDtypeStruct((b, s, POOL_WIDTH), F32),
        jax.ShapeDtypeStruct((b, N_HEADS, s, V_HEAD), BF16),
        jax.ShapeDtypeStruct((b, N_HEADS, s, V_HEAD), F32),
        jax.ShapeDtypeStruct((b, N_HEADS, s, V_HEAD), F32),
        jax.ShapeDtypeStruct((b, s, ATTN_WIDTH), F32),
    ]
    out_specs = [
        pl.BlockSpec((1, tm, COL_TILE), row_tile),
        pl.BlockSpec((1, tm, COL_TILE), row_tile),
        pl.BlockSpec(head_block, head_map(SEG_Q)),
        pl.BlockSpec(head_block, head_map(SEG_K)),
        pl.BlockSpec(head_block, head_map(SEG_V)),
        pl.BlockSpec((1, tm, COL_TILE), lambda bi, i, j: (bi, i, seg(SEG_GA, j))),
    ]
    n_f32_tiles, n_bf16_tiles = 5, 1
    if attn_layouts:
        out_shape += [
            jax.ShapeDtypeStruct((b, N_HEADS, s, V_HEAD), BF16),
            jax.ShapeDtypeStruct((b, N_HEADS, s, V_HEAD), BF16),
            jax.ShapeDtypeStruct((b, N_HEADS, V_HEAD, s), BF16),
        ]
        out_specs += [
            pl.BlockSpec(head_block, head_map(SEG_K)),
            pl.BlockSpec(head_block, head_map(SEG_K)),
            pl.BlockSpec((1, HEADS_PER_TILE, V_HEAD, tm), lambda bi, i, j: (bi, seg(SEG_V, j), 0, i)),
        ]
        n_bf16_tiles += 3
    vmem = (tm * d * 4 + tm * d * 2 + 2 * d * COL_TILE * 2
            + 2 * tm * COL_TILE * (n_f32_tiles * 4 + n_bf16_tiles * 2) + 3 * tm * COL_TILE * 4)
    return pl.pallas_call(
        functools.partial(_proj_kernel, tm=tm, attn_layouts=attn_layouts),
        grid=grid,
        in_specs=[
            pl.BlockSpec((1, tm, d), row_tile, pipeline_mode=pl.Buffered(1)),
            pl.BlockSpec((1, d), lambda bi, i, j: (0, 0)),
            pl.BlockSpec((d, COL_TILE), lambda bi, i, j: (0, j)),
        ],
        out_specs=out_specs,
        out_shape=out_shape,
        scratch_shapes=[pltpu.VMEM((tm, d), BF16)],
        compiler_params=pltpu.CompilerParams(
            dimension_semantics=("arbitrary", "arbitrary", "arbitrary"),
            vmem_limit_bytes=vmem + 4 * MIB,
        ),
        name="proj",
    )(x, norm_w.reshape(1, d), w_in_bf16)


def _pooled_groups(u_ref, prev_ref, hist_ref, wp_ref, ps_ref, ext_scr, *, tm, hist_valid):
    i = pl.program_id(1)
    ext_scr[0:HALO, :] = jnp.where(i == 0, hist_ref[0], prev_ref[0])
    ext_scr[HALO:HALO + tm, :] = u_ref[0]
    frame = i * tm + lax.broadcasted_iota(jnp.int32, (tm, POOL_GROUP_DIM), 0)
    out = []
    for g, w in enumerate(POOL_WINDOWS):
        lanes = slice(g * POOL_GROUP_DIM, (g + 1) * POOL_GROUP_DIM)
        cur = ext_scr[HALO:HALO + tm, lanes]
        acc = cur
        for dlt in range(1, w):
            acc = acc + ext_scr[HALO - dlt:HALO - dlt + tm, lanes]
        count = jnp.minimum(w, hist_valid + frame + 1).astype(F32)
        diff = acc / count - cur
        y = jnp.dot(diff.astype(BF16), wp_ref[g], preferred_element_type=F32)
        out.append(y * ps_ref[:, lanes])
    return out


def _pool_in_specs(tm, c):
    blocks_per_tile = tm // HALO
    return [
        pl.BlockSpec((1, tm, c), lambda bi, i: (bi, i, 0)),
        pl.BlockSpec((1, HALO, c), lambda bi, i: (bi, jnp.maximum(i * blocks_per_tile - 1, 0), 0)),
        pl.BlockSpec((1, HALO, c), lambda bi, i: (bi, 0, 0)),
        pl.BlockSpec((len(POOL_WINDOWS), POOL_GROUP_DIM, POOL_GROUP_DIM), lambda bi, i: (0, 0, 0)),
        pl.BlockSpec((1, c), lambda bi, i: (0, 0)),
    ]


def _pool_kernel(u_ref, prev_ref, hist_ref, wp_ref, ps_ref, o_ref, ext_scr, *, tm, hist_valid):
    groups = _pooled_groups(u_ref, prev_ref, hist_ref, wp_ref, ps_ref, ext_scr, tm=tm, hist_valid=hist_valid)
    for g, y in enumerate(groups):
        o_ref[0, :, g * POOL_GROUP_DIM:(g + 1) * POOL_GROUP_DIM] = y


def _pool_mix(u, hist, w_pool_bf16, pool_scale, *, tm, hist_valid):
    b, t, c = u.shape
    assert t % tm == 0 and tm % HALO == 0
    return pl.pallas_call(
        functools.partial(_pool_kernel, tm=tm, hist_valid=hist_valid),
        grid=(b, t // tm),
        in_specs=_pool_in_specs(tm, c),
        out_specs=pl.BlockSpec((1, tm, c), lambda bi, i: (bi, i, 0)),
        out_shape=jax.ShapeDtypeStruct((b, t, c), F32),
        scratch_shapes=[pltpu.VMEM((HALO + tm, c), F32)],
        compiler_params=pltpu.CompilerParams(dimension_semantics=("arbitrary", "arbitrary")),
        name="pool",
    )(u, u, hist, w_pool_bf16, pool_scale.reshape(1, c))


ACC_ROWS = V_HEAD + BF16_SUBLANES
KEY_BLOCK_UNROLLS = (4, 2, 1)


def _attn_kernel(sig_ref, q_ref, ka_ref, kb_ref, vt_ref, ga_ref, lq1_ref, lk1_ref, lq2_ref, lk2_ref, sw_ref, o_ref,
                 qp_scr, qfeat_scr, bias_scr, sh_scr, mb_scr, m_scr, acc_scr, *, lam_init):
    tq = KEY_BLOCK
    h = pl.program_id(1)
    qi = pl.program_id(2)
    sigma = sig_ref[h]
    decay = sigma * tq
    k_refs = (ka_ref, kb_ref)

    @pl.when(qi == 0)
    def _():
        kj = lax.broadcasted_iota(jnp.int32, (tq, tq), 0)
        qq = lax.broadcasted_iota(jnp.int32, (tq, tq), 1)
        rel = (qq - kj).astype(F32)
        allowed = lax.shift_right_logical(kj, CHUNK_SHIFT) <= lax.shift_right_logical(qq, CHUNK_SHIFT)
        bias_scr[...] = jnp.where(allowed, jnp.where(rel >= 0.0, 0.0, 2.0 * sigma * rel), MASK_VALUE)
        row = lax.broadcasted_iota(jnp.int32, (V_HEAD, tq), 0)
        col = lax.broadcasted_iota(jnp.int32, (V_HEAD, tq), 1).astype(F32)
        parts = _bf16_parts(jnp.full((V_HEAD, tq), sigma, F32)) * 2 + _bf16_parts(-sigma * col)
        assert len(parts) == N_POS_FEATURES
        for c, slot in enumerate((row - D_HEAD, row)):
            feat = jnp.zeros((V_HEAD, tq), F32)
            for n, part in enumerate(parts):
                feat = jnp.where(slot == n, part, feat)
            qfeat_scr[c] = feat

    qt = q_ref[0, 0].astype(F32).T
    row = lax.broadcasted_iota(jnp.int32, (V_HEAD, tq), 0)
    qp_scr[0] = jnp.where(row < D_HEAD, qt, qfeat_scr[0]).astype(BF16)
    qp_scr[1] = jnp.where(row >= D_HEAD, qt, qfeat_scr[1]).astype(BF16)
    m_scr[...] = jnp.full(m_scr.shape, MASK_VALUE, F32)
    acc_scr[...] = jnp.zeros(acc_scr.shape, F32)
    ones_rows = jnp.ones((BF16_SUBLANES, tq), BF16)

    def step(value_kb, score_kb, diagonal=False):
        if score_kb is not None:
            k0 = pl.multiple_of(score_kb * tq, tq)
        if value_kb is not None:
            v0 = pl.multiple_of(value_kb * tq, tq)
            vblk = jnp.concatenate([vt_ref[0, 0, :, pl.ds(v0, tq)], ones_rows], axis=0)
        for c in range(2):
            if value_kb is not None:
                m_old = m_scr[c] - decay
                m_new = jnp.maximum(m_old, mb_scr[c])
                p = jnp.exp2(sh_scr[c] - m_new).astype(BF16)
                alpha = jnp.exp2(m_old - m_new)
            if score_kb is not None:
                kblk = k_refs[c][0, 0, pl.ds(k0, tq), :]
                sh = jnp.dot(kblk, qp_scr[c], preferred_element_type=F32)
                if diagonal == "first_query_block":
                    sh = sh + jnp.where(qi == 0, bias_scr[...], 0.0)
                elif diagonal:
                    sh = sh + bias_scr[...]
                sh_scr[c] = sh
                mb_scr[c] = jnp.max(sh, axis=0, keepdims=True)
            if value_kb is not None:
                acc_scr[c] = alpha * acc_scr[c] + jnp.dot(vblk, p, preferred_element_type=F32)
                m_scr[c] = m_new

    step(None, 0, diagonal="first_query_block")

    n_plain = jnp.maximum(qi - 1, 0)
    done = 0
    for unroll in KEY_BLOCK_UNROLLS:
        trips = (n_plain - done) // unroll

        def body(i, carry, unroll=unroll, base=done):
            for u in range(unroll):
                step(base + i * unroll + u, base + i * unroll + u + 1)
            return carry

        lax.fori_loop(0, trips, body, 0)
        done = done + trips * unroll

    def finish():
        step(qi, None)
        lam = _lam(lq1_ref, lk1_ref, lq2_ref, lk2_ref, lam_init)
        o1 = acc_scr[0, 0:V_HEAD, :] / acc_scr[0, V_HEAD:V_HEAD + 1, :]
        o2 = acc_scr[1, 0:V_HEAD, :] / acc_scr[1, V_HEAD:V_HEAD + 1, :]
        o = o1 - lam * o2
        o = o * lax.rsqrt(jnp.mean(o * o, axis=0, keepdims=True) + SUBLN_EPS)
        attn = o.T * sw_ref[...] * (1.0 - lam_init)
        o_ref[0] = (jax.nn.silu(ga_ref[0]) * attn).astype(BF16)

    @pl.when(qi > 0)
    def _():
        step(qi - 1, qi, diagonal=True)
        finish()

    @pl.when(qi == 0)
    def _():
        finish()


def _prompt_attention(q, ka, kb, vt, g_attn, lams, subln_w, *, lam_init):
    b, _, s, _ = q.shape
    tq = KEY_BLOCK
    assert s % tq == 0 and tq % CHUNK == 0
    lam_spec = pl.BlockSpec((1, D_HEAD), lambda bi, h, i: (0, 0))
    whole_seq = lambda bi, h, i: (bi, h, 0, 0)
    vmem = 2 * 3 * s * V_HEAD * 2 + (1 + 2) * tq * tq * 4 + 8 * tq * tq * 4 + 12 * tq * ACC_ROWS * 4
    return pl.pallas_call(
        functools.partial(_attn_kernel, lam_init=lam_init),
        grid=(b, N_HEADS, s // tq),
        in_specs=[
            pl.BlockSpec(memory_space=pltpu.SMEM),
            pl.BlockSpec((1, 1, tq, V_HEAD), lambda bi, h, i: (bi, h, i, 0)),
            pl.BlockSpec((1, 1, s, V_HEAD), whole_seq),
            pl.BlockSpec((1, 1, s, V_HEAD), whole_seq),
            pl.BlockSpec((1, 1, V_HEAD, s), whole_seq),
            pl.BlockSpec((1, tq, V_HEAD), lambda bi, h, i: (bi, i, h)),
            lam_spec, lam_spec, lam_spec, lam_spec,
            pl.BlockSpec((1, V_HEAD), lambda bi, h, i: (0, 0)),
        ],
        out_specs=pl.BlockSpec((1, tq, V_HEAD), lambda bi, h, i: (bi, i, h)),
        out_shape=jax.ShapeDtypeStruct((b, s, ATTN_WIDTH), BF16),
        scratch_shapes=[
            pltpu.VMEM((2, V_HEAD, tq), BF16),
            pltpu.VMEM((2, V_HEAD, tq), F32),
            pltpu.VMEM((tq, tq), F32),
            pltpu.VMEM((2, tq, tq), F32),
            pltpu.VMEM((2, 1, tq), F32),
            pltpu.VMEM((2, 1, tq), F32),
            pltpu.VMEM((2, ACC_ROWS, tq), F32),
        ],
        compiler_params=pltpu.CompilerParams(
            dimension_semantics=("arbitrary", "arbitrary", "arbitrary"),
            vmem_limit_bytes=vmem + 8 * MIB,
        ),
        name="attn",
    )(_alibi_sigmas(), q, ka, kb, vt, g_attn, *lams, subln_w.reshape(1, V_HEAD))


def _decode_kernel(sig_ref, q_ref, kn_ref, vn_ref, kc_ref, vc_ref, ga_ref, lq1_ref, lk1_ref, lq2_ref, lk2_ref,
                   sw_ref, o_ref, *, t, past, lam_init):
    hg = pl.program_id(1)
    lam = _lam(lq1_ref, lk1_ref, lq2_ref, lk2_ref, lam_init)
    qpos_c = past + lax.broadcasted_iota(jnp.int32, (t, past), 0)
    kpos_c = lax.broadcasted_iota(jnp.int32, (t, past), 1)
    qpos_n = past + lax.broadcasted_iota(jnp.int32, (t, t), 0)
    kpos_n = past + lax.broadcasted_iota(jnp.int32, (t, t), 1)

    def bias(qpos, kpos, sigma):
        allowed = lax.shift_right_logical(kpos, CHUNK_SHIFT) <= lax.shift_right_logical(qpos, CHUNK_SHIFT)
        return jnp.where(allowed, -sigma * jnp.abs(qpos - kpos).astype(F32), MASK_VALUE)

    lane = lax.broadcasted_iota(jnp.int32, (t, V_HEAD), 1)
    nt = (((1,), (1,)), ((), ()))
    for hh in range(HEADS_PER_TILE):
        sigma = sig_ref[hg * HEADS_PER_TILE + hh]
        cols = slice(hh * V_HEAD, (hh + 1) * V_HEAD)
        q = q_ref[0, hh]
        kc = kc_ref[0, hh].astype(BF16)
        vc = vc_ref[0, hh].astype(BF16)
        kn = kn_ref[0, hh].astype(BF16)
        vn = vn_ref[0, hh].astype(BF16)
        bias_c = bias(qpos_c, kpos_c, sigma)
        bias_n = bias(qpos_n, kpos_n, sigma)
        outs = []
        for c in range(2):
            keep = (lane < D_HEAD) if c == 0 else (lane >= D_HEAD)
            qc = jnp.where(keep, q, jnp.zeros_like(q))
            s_c = lax.dot_general(qc, kc, nt, preferred_element_type=F32) + bias_c
            s_n = lax.dot_general(qc, kn, nt, preferred_element_type=F32) + bias_n
            m = jnp.maximum(jnp.max(s_c, axis=-1, keepdims=True), jnp.max(s_n, axis=-1, keepdims=True))
            p_c = jnp.exp2(s_c - m)
            p_n = jnp.exp2(s_n - m)
            l = jnp.sum(p_c, axis=-1, keepdims=True) + jnp.sum(p_n, axis=-1, keepdims=True)
            pv = (jnp.dot(p_c.astype(BF16), vc, preferred_element_type=F32)
                  + jnp.dot(p_n.astype(BF16), vn, preferred_element_type=F32))
            outs.append(pv / l)
        o = outs[0] - lam * outs[1]
        o = o * lax.rsqrt(jnp.mean(o * o, axis=-1, keepdims=True) + SUBLN_EPS)
        attn = o * sw_ref[...] * (1.0 - lam_init)
        o_ref[0, :, cols] = (jax.nn.silu(ga_ref[0, :, cols]) * attn).astype(BF16)


def _decode_attention(q, k_new, v_new, cache_k, cache_v, g_attn, lams, subln_w, *, db, lam_init):
    rows = q.shape[2]
    t = rows // db
    past = cache_k.shape[2]
    new_spec = pl.BlockSpec((1, HEADS_PER_TILE, t, V_HEAD), lambda bi, g: (0, g, bi, 0))
    cache_spec = pl.BlockSpec((1, HEADS_PER_TILE, past, V_HEAD), lambda bi, g: (bi, g, 0, 0))
    row_spec = pl.BlockSpec((1, t, COL_TILE), lambda bi, g: (0, bi, g))
    lam_spec = pl.BlockSpec((1, D_HEAD), lambda bi, g: (0, 0))
    vmem = 2 * 2 * past * COL_TILE * 4 + 8 * past * V_HEAD * 4 + 16 * t * past * 4
    return pl.pallas_call(
        functools.partial(_decode_kernel, t=t, past=past, lam_init=lam_init),
        grid=(db, N_HEADS // HEADS_PER_TILE),
        in_specs=[
            pl.BlockSpec(memory_space=pltpu.SMEM),
            new_spec, new_spec, new_spec, cache_spec, cache_spec, row_spec,
            lam_spec, lam_spec, lam_spec, lam_spec,
            pl.BlockSpec((1, V_HEAD), lambda bi, g: (0, 0)),
        ],
        out_specs=row_spec,
        out_shape=jax.ShapeDtypeStruct((1, rows, ATTN_WIDTH), BF16),
        compiler_params=pltpu.CompilerParams(
            dimension_semantics=("arbitrary", "arbitrary"),
            vmem_limit_bytes=vmem + 8 * MIB,
        ),
        name="decode_attn",
    )(_alibi_sigmas(), q, k_new, v_new, cache_k, cache_v, g_attn, *lams, subln_w.reshape(1, V_HEAD))


def _out_kernel(pool_ref, gp_ref, mix_a_ref, x_ref, wo_ref, fw_ref, y_ref):
    mix_p = (jax.nn.silu(gp_ref[0]) * pool_ref[0]).astype(BF16)
    r = jnp.dot(mix_p, wo_ref[0:POOL_WIDTH, :], preferred_element_type=F32)
    r = r + jnp.dot(mix_a_ref[0], wo_ref[POOL_WIDTH:, :], preferred_element_type=F32)
    hres = x_ref[0] + r
    ms = jnp.mean(hres * hres, axis=-1, keepdims=True)
    y_ref[0] = hres * lax.rsqrt(ms + NORM_EPS) * fw_ref[...]


def _merge_out(pool_out, g_pool, mix_attn, x, w_out_bf16, final_norm_w, *, tm):
    b, s, d = x.shape
    assert s % tm == 0
    row = lambda width: pl.BlockSpec((1, tm, width), lambda bi, i: (bi, i, 0))
    vmem = 2 * tm * ((2 * POOL_WIDTH + 2 * d) * 4 + ATTN_WIDTH * 2) + 2 * d * d * 2 + 6 * tm * d * 4
    return pl.pallas_call(
        _out_kernel,
        grid=(b, s // tm),
        in_specs=[
            row(POOL_WIDTH), row(POOL_WIDTH), row(ATTN_WIDTH), row(d),
            pl.BlockSpec((d, d), lambda bi, i: (0, 0)),
            pl.BlockSpec((1, d), lambda bi, i: (0, 0)),
        ],
        out_specs=row(d),
        out_shape=jax.ShapeDtypeStruct((b, s, d), F32),
        compiler_params=pltpu.CompilerParams(
            dimension_semantics=("arbitrary", "arbitrary"),
            vmem_limit_bytes=vmem + 8 * MIB,
        ),
        name="merge_out",
    )(pool_out, g_pool, mix_attn, x, w_out_bf16, final_norm_w.reshape(1, d))


def _tile(n, target):
    t = min(n, target)
    while n % t:
        t //= 2
    return t


def _last_rows(hist, u):
    if u.shape[1] >= POOL_HIST:
        return u[:, u.shape[1] - POOL_HIST:]
    return jnp.concatenate([hist, u], axis=1)[:, -POOL_HIST:]


def kernel(x_prompt, x_sample, cache_k, cache_v, state_pool, norm_w, w_in, w_pool, pool_scale,
           lambda_q1, lambda_k1, lambda_q2, lambda_k2, subln_w, w_out, final_norm_w):
    depth = norm_w.shape[0]
    assert depth == 1, "the final norm is fused into the single layer's output kernel"
    layer = 0
    lam_init = _lam_init(layer)
    b, s, d = x_prompt.shape
    db, t, _ = x_sample.shape
    assert s % KEY_BLOCK == 0

    w_in_b = w_in[layer].astype(BF16)
    w_out_b = w_out[layer].astype(BF16)
    w_pool_b = w_pool[layer].astype(BF16)
    lams = tuple(p[layer].reshape(1, D_HEAD) for p in (lambda_q1, lambda_k1, lambda_q2, lambda_k2))
    swap_head_axis = lambda a: jnp.transpose(a, (0, 2, 1, 3))

    u, gp, q, k, v, ga, ka, kb, vt = _project(x_prompt, norm_w[layer], w_in_b, tm=_tile(s, 1024), attn_layouts=True)
    hist0 = jnp.zeros((b, HALO, POOL_WIDTH), F32)
    pool_out = _pool_mix(u, hist0, w_pool_b, pool_scale[layer], tm=_tile(s, 512), hist_valid=0)
    mix_attn = _prompt_attention(q, ka, kb, vt, ga, lams, subln_w[layer], lam_init=lam_init)
    y_prompt = _merge_out(pool_out, gp, mix_attn, x_prompt, w_out_b, final_norm_w, tm=_tile(s, 256))

    xs = x_sample.reshape(1, db * t, d)
    u2, gp2, q2, k2, v2, ga2 = _project(xs, norm_w[layer], w_in_b, tm=_tile(db * t, 512), attn_layouts=False)
    hist = jnp.concatenate([jnp.zeros((db, HALO - POOL_HIST, POOL_WIDTH), F32), state_pool[layer]], axis=1)
    u2s = u2.reshape(db, t, POOL_WIDTH)
    pool_out2 = _pool_mix(u2s, hist, w_pool_b, pool_scale[layer], tm=t, hist_valid=POOL_HIST)
    mix_attn2 = _decode_attention(q2, k2, v2, swap_head_axis(cache_k[layer]), swap_head_axis(cache_v[layer]), ga2,
                                  lams, subln_w[layer], db=db, lam_init=lam_init)
    y_sample = _merge_out(pool_out2.reshape(1, db * t, POOL_WIDTH), gp2, mix_attn2, xs, w_out_b, final_norm_w,
                          tm=_tile(db * t, 256)).reshape(db, t, d)

    per_stream = lambda a: swap_head_axis(a).reshape(db, t, N_HEADS, V_HEAD)[None]
    return (y_prompt, y_sample, swap_head_axis(k)[None], swap_head_axis(v)[None], _last_rows(hist0[:, 1:], u)[None],
            per_stream(k2), per_stream(v2), _last_rows(state_pool[layer], u2s)[None])
```

```python
import functools
import math

import jax
import jax.numpy as jnp
import numpy as np
from jax import lax
from jax.experimental import pallas as pl
from jax.experimental.pallas import tpu as pltpu

F32 = jnp.float32
BF16 = jnp.bfloat16

D_MODEL = 2048
CHUNK = 64
CHUNK_SHIFT = 6
POOL_WINDOWS = (2, 4, 8, 16)
POOL_GROUP_DIM = 128
POOL_WIDTH = len(POOL_WINDOWS) * POOL_GROUP_DIM
POOL_HIST = max(POOL_WINDOWS) - 1
HALO = POOL_HIST + 1
ATTN_WIDTH = D_MODEL - POOL_WIDTH
N_HEADS = 12
V_HEAD = ATTN_WIDTH // N_HEADS
D_HEAD = V_HEAD // 2
IN_WIDTH = 2 * POOL_WIDTH + 4 * ATTN_WIDTH
NORM_EPS = 1e-6
SUBLN_EPS = 1e-5
MASK_VALUE = -1e30
LOG2E = math.log2(math.e)
Q_SCALE = (D_HEAD ** -0.5) * LOG2E

COL_TILE = 512
N_COL_TILES = IN_WIDTH // COL_TILE
ATTN_COL_TILES = ATTN_WIDTH // COL_TILE
HEADS_PER_TILE = COL_TILE // V_HEAD
SEG_GP, SEG_Q, SEG_K, SEG_V, SEG_GA = 1, 2, 2 + ATTN_COL_TILES, 2 + 2 * ATTN_COL_TILES, 2 + 3 * ATTN_COL_TILES

KEY_BLOCK = 512
N_POS_FEATURES = 9
BF16_SUBLANES = 16
MIB = 1024 * 1024


def _alibi_sigmas():
    def pow2_slopes(m):
        start = 2.0 ** (-8.0 / m)
        return [start ** (i + 1) for i in range(m)]

    n = N_HEADS
    if math.log2(n).is_integer():
        s = pow2_slopes(n)
    else:
        c = 2 ** int(math.floor(math.log2(n)))
        s = pow2_slopes(c) + pow2_slopes(2 * c)[0::2][: n - c]
    return jnp.asarray(np.array(s, dtype=np.float32) * np.float32(LOG2E))


def _lam_init(layer):
    return 0.8 - 0.6 * math.exp(-0.3 * layer)


def _lam(lq1_ref, lk1_ref, lq2_ref, lk2_ref, lam_init):
    a = jnp.sum(lq1_ref[...] * lk1_ref[...], axis=-1, keepdims=True)
    b = jnp.sum(lq2_ref[...] * lk2_ref[...], axis=-1, keepdims=True)
    return jnp.exp(a) - jnp.exp(b) + lam_init


def _bf16_parts(x):
    p1 = x.astype(BF16).astype(F32)
    p2 = (x - p1).astype(BF16).astype(F32)
    p3 = (x - p1 - p2).astype(BF16).astype(F32)
    return (p1, p2, p3)


def _position_features(kj, slot):
    hi = lax.shift_left(lax.shift_right_logical(kj, 4), 4).astype(F32)
    lo = (kj & 15).astype(F32)
    return jnp.where(slot < 0, 0.0, jnp.where(slot < 3, hi, jnp.where(slot < 6, lo, jnp.where(slot < 9, 1.0, 0.0))))


def _retile_kernel(w_ref, o_ref):
    o_ref[0] = w_ref[...].astype(BF16)


def _retile_w_in(w_in):
    d = w_in.shape[0]
    return pl.pallas_call(
        _retile_kernel,
        grid=(N_COL_TILES,),
        in_specs=[pl.BlockSpec((d, COL_TILE), lambda j: (0, j))],
        out_specs=pl.BlockSpec((1, d, COL_TILE), lambda j: (j, 0, 0)),
        out_shape=jax.ShapeDtypeStruct((N_COL_TILES, d, COL_TILE), BF16),
        compiler_params=pltpu.CompilerParams(dimension_semantics=("arbitrary",)),
        name="retile_w_in",
    )(w_in)


def _proj_kernel(x_ref, nw_ref, w_ref, u_ref, gp_ref, q_ref, k_ref, v_ref, ga_ref, *rest, tm, attn_layouts):
    if attn_layouts:
        ka_ref, kb_ref, vt_ref, h_scr = rest
    else:
        (h_scr,) = rest
    i = pl.program_id(1)
    j = pl.program_id(2)

    @pl.when(j == 0)
    def _():
        x = x_ref[0]
        ms = jnp.mean(x * x, axis=-1, keepdims=True)
        h_scr[...] = (x * lax.rsqrt(ms + NORM_EPS) * nw_ref[...]).astype(BF16)

    def tile():
        return jnp.dot(h_scr[...], w_ref[0], preferred_element_type=F32)

    @pl.when(j == 0)
    def _():
        u_ref[0] = tile()

    @pl.when(j == SEG_GP)
    def _():
        gp_ref[0] = tile()

    @pl.when((j >= SEG_Q) & (j < SEG_K))
    def _():
        z = tile()
        for hh in range(HEADS_PER_TILE):
            q_ref[0, hh] = (z[:, hh * V_HEAD:(hh + 1) * V_HEAD] * Q_SCALE).astype(BF16)

    @pl.when((j >= SEG_K) & (j < SEG_V))
    def _():
        z = tile()
        if attn_layouts:
            lane = lax.broadcasted_iota(jnp.int32, (tm, V_HEAD), 1)
            kj = (i * tm + lax.broadcasted_iota(jnp.int32, (tm, V_HEAD), 0)) & (KEY_BLOCK - 1)
            feat_a = _position_features(kj, lane - D_HEAD)
            feat_b = _position_features(kj, lane)
        for hh in range(HEADS_PER_TILE):
            zh = z[:, hh * V_HEAD:(hh + 1) * V_HEAD]
            k_ref[0, hh] = zh
            if attn_layouts:
                ka_ref[0, hh] = jnp.where(lane < D_HEAD, zh, feat_a).astype(BF16)
                kb_ref[0, hh] = jnp.where(lane >= D_HEAD, zh, feat_b).astype(BF16)

    @pl.when((j >= SEG_V) & (j < SEG_GA))
    def _():
        z = tile()
        for hh in range(HEADS_PER_TILE):
            zh = z[:, hh * V_HEAD:(hh + 1) * V_HEAD]
            v_ref[0, hh] = zh
            if attn_layouts:
                vt_ref[0, hh] = zh.T.astype(BF16)

    @pl.when(j >= SEG_GA)
    def _():
        z = tile()
        for hh in range(HEADS_PER_TILE):
            ga_ref[0, hh] = z[:, hh * V_HEAD:(hh + 1) * V_HEAD]


def _project(x, norm_w, w_in_bf16, *, tm, attn_layouts):
    b, s, d = x.shape
    assert s % tm == 0 and d == D_MODEL
    grid = (b, s // tm, N_COL_TILES)

    def seg(first, j):
        return jnp.clip(j - first, 0, ATTN_COL_TILES - 1)

    row_tile = lambda bi, i, j: (bi, i, 0)
    head_block = (1, HEADS_PER_TILE, tm, V_HEAD)
    head_map = lambda first: (lambda bi, i, j: (bi, seg(first, j), i, 0))
    out_shape = [
        jax.ShapeDtypeStruct((b, s, POOL_WIDTH), F32),
        jax.ShapeDtypeStruct((b, s, POOL_WIDTH), F32),
        jax.ShapeDtypeStruct((b, N_HEADS, s, V_HEAD), BF16),
        jax.ShapeDtypeStruct((b, N_HEADS, s, V_HEAD), F32),
        jax.ShapeDtypeStruct((b, N_HEADS, s, V_HEAD), F32),
        jax.ShapeDtypeStruct((b, N_HEADS, s, V_HEAD), F32),
    ]
    out_specs = [
        pl.BlockSpec((1, tm, COL_TILE), row_tile),
        pl.BlockSpec((1, tm, COL_TILE), row_tile),
        pl.BlockSpec(head_block, head_map(SEG_Q)),
        pl.BlockSpec(head_block, head_map(SEG_K)),
        pl.BlockSpec(head_block, head_map(SEG_V)),
        pl.BlockSpec(head_block, head_map(SEG_GA)),
    ]
    n_f32_tiles, n_bf16_tiles = 5, 1
    if attn_layouts:
        out_shape += [
            jax.ShapeDtypeStruct((b, N_HEADS, s, V_HEAD), BF16),
            jax.ShapeDtypeStruct((b, N_HEADS, s, V_HEAD), BF16),
            jax.ShapeDtypeStruct((b, N_HEADS, V_HEAD, s), BF16),
        ]
        out_specs += [
            pl.BlockSpec(head_block, head_map(SEG_K)),
            pl.BlockSpec(head_block, head_map(SEG_K)),
            pl.BlockSpec((1, HEADS_PER_TILE, V_HEAD, tm), lambda bi, i, j: (bi, seg(SEG_V, j), 0, i)),
        ]
        n_bf16_tiles += 3
    vmem = (tm * d * 4 + tm * d * 2 + 2 * d * COL_TILE * 2
            + 2 * tm * COL_TILE * (n_f32_tiles * 4 + n_bf16_tiles * 2) + 3 * tm * COL_TILE * 4)
    return pl.pallas_call(
        functools.partial(_proj_kernel, tm=tm, attn_layouts=attn_layouts),
        grid=grid,
        in_specs=[
            pl.BlockSpec((1, tm, d), row_tile, pipeline_mode=pl.Buffered(1)),
            pl.BlockSpec((1, d), lambda bi, i, j: (0, 0)),
            pl.BlockSpec((1, d, COL_TILE), lambda bi, i, j: (j, 0, 0)),
        ],
        out_specs=out_specs,
        out_shape=out_shape,
        scratch_shapes=[pltpu.VMEM((tm, d), BF16)],
        compiler_params=pltpu.CompilerParams(
            dimension_semantics=("arbitrary", "arbitrary", "arbitrary"),
            vmem_limit_bytes=vmem + 4 * MIB,
        ),
        name="proj",
    )(x, norm_w.reshape(1, d), w_in_bf16)


def _pooled_groups(u_ref, prev_ref, hist_ref, wp_ref, ps_ref, ext_scr, *, tm, hist_valid):
    i = pl.program_id(1)
    ext_scr[0:HALO, :] = jnp.where(i == 0, hist_ref[0], prev_ref[0])
    ext_scr[HALO:HALO + tm, :] = u_ref[0]
    frame = i * tm + lax.broadcasted_iota(jnp.int32, (tm, POOL_GROUP_DIM), 0)
    out = []
    for g, w in enumerate(POOL_WINDOWS):
        lanes = slice(g * POOL_GROUP_DIM, (g + 1) * POOL_GROUP_DIM)
        cur = ext_scr[HALO:HALO + tm, lanes]
        acc = cur
        for dlt in range(1, w):
            acc = acc + ext_scr[HALO - dlt:HALO - dlt + tm, lanes]
        count = jnp.minimum(w, hist_valid + frame + 1).astype(F32)
        diff = acc / count - cur
        y = jnp.dot(diff.astype(BF16), wp_ref[g], preferred_element_type=F32)
        out.append(y * ps_ref[:, lanes])
    return out


def _pool_in_specs(tm, c):
    blocks_per_tile = tm // HALO
    return [
        pl.BlockSpec((1, tm, c), lambda bi, i: (bi, i, 0)),
        pl.BlockSpec((1, HALO, c), lambda bi, i: (bi, jnp.maximum(i * blocks_per_tile - 1, 0), 0)),
        pl.BlockSpec((1, HALO, c), lambda bi, i: (bi, 0, 0)),
        pl.BlockSpec((len(POOL_WINDOWS), POOL_GROUP_DIM, POOL_GROUP_DIM), lambda bi, i: (0, 0, 0)),
        pl.BlockSpec((1, c), lambda bi, i: (0, 0)),
    ]


def _pool_kernel(u_ref, prev_ref, hist_ref, wp_ref, ps_ref, o_ref, ext_scr, *, tm, hist_valid):
    groups = _pooled_groups(u_ref, prev_ref, hist_ref, wp_ref, ps_ref, ext_scr, tm=tm, hist_valid=hist_valid)
    for g, y in enumerate(groups):
        o_ref[0, :, g * POOL_GROUP_DIM:(g + 1) * POOL_GROUP_DIM] = y


def _pool_mix(u, hist, w_pool_bf16, pool_scale, *, tm, hist_valid):
    b, t, c = u.shape
    assert t % tm == 0 and tm % HALO == 0
    return pl.pallas_call(
        functools.partial(_pool_kernel, tm=tm, hist_valid=hist_valid),
        grid=(b, t // tm),
        in_specs=_pool_in_specs(tm, c),
        out_specs=pl.BlockSpec((1, tm, c), lambda bi, i: (bi, i, 0)),
        out_shape=jax.ShapeDtypeStruct((b, t, c), F32),
        scratch_shapes=[pltpu.VMEM((HALO + tm, c), F32)],
        compiler_params=pltpu.CompilerParams(dimension_semantics=("arbitrary", "arbitrary")),
        name="pool",
    )(u, u, hist, w_pool_bf16, pool_scale.reshape(1, c))


ACC_ROWS = V_HEAD + BF16_SUBLANES
KEY_BLOCK_UNROLLS = (4, 2, 1)


def _attn_kernel(sig_ref, q_ref, ka_ref, kb_ref, vt_ref, ga_ref, lq1_ref, lk1_ref, lq2_ref, lk2_ref, sw_ref, o_ref,
                 qp_scr, qfeat_scr, bias_scr, sh_scr, mb_scr, m_scr, acc_scr, *, lam_init):
    tq = KEY_BLOCK
    h = pl.program_id(1)
    qi = pl.program_id(2)
    sigma = sig_ref[h]
    decay = sigma * tq
    k_refs = (ka_ref, kb_ref)

    @pl.when(qi == 0)
    def _():
        kj = lax.broadcasted_iota(jnp.int32, (tq, tq), 0)
        qq = lax.broadcasted_iota(jnp.int32, (tq, tq), 1)
        rel = (qq - kj).astype(F32)
        allowed = lax.shift_right_logical(kj, CHUNK_SHIFT) <= lax.shift_right_logical(qq, CHUNK_SHIFT)
        bias_scr[...] = jnp.where(allowed, jnp.where(rel >= 0.0, 0.0, 2.0 * sigma * rel), MASK_VALUE)
        row = lax.broadcasted_iota(jnp.int32, (V_HEAD, tq), 0)
        col = lax.broadcasted_iota(jnp.int32, (V_HEAD, tq), 1).astype(F32)
        parts = _bf16_parts(jnp.full((V_HEAD, tq), sigma, F32)) * 2 + _bf16_parts(-sigma * col)
        assert len(parts) == N_POS_FEATURES
        for c, slot in enumerate((row - D_HEAD, row)):
            feat = jnp.zeros((V_HEAD, tq), F32)
            for n, part in enumerate(parts):
                feat = jnp.where(slot == n, part, feat)
            qfeat_scr[c] = feat

    qt = q_ref[0, 0].astype(F32).T
    row = lax.broadcasted_iota(jnp.int32, (V_HEAD, tq), 0)
    qp_scr[0] = jnp.where(row < D_HEAD, qt, qfeat_scr[0]).astype(BF16)
    qp_scr[1] = jnp.where(row >= D_HEAD, qt, qfeat_scr[1]).astype(BF16)
    m_scr[...] = jnp.full(m_scr.shape, MASK_VALUE, F32)
    acc_scr[...] = jnp.zeros(acc_scr.shape, F32)
    ones_rows = jnp.ones((BF16_SUBLANES, tq), BF16)

    def step(value_kb, score_kb, diagonal=False):
        if score_kb is not None:
            k0 = pl.multiple_of(score_kb * tq, tq)
        if value_kb is not None:
            v0 = pl.multiple_of(value_kb * tq, tq)
            vblk = jnp.concatenate([vt_ref[0, 0, :, pl.ds(v0, tq)], ones_rows], axis=0)
        for c in range(2):
            if value_kb is not None:
                m_old = m_scr[c] - decay
                m_new = jnp.maximum(m_old, mb_scr[c])
                p = jnp.exp2(sh_scr[c] - m_new).astype(BF16)
                alpha = jnp.exp2(m_old - m_new)
            if score_kb is not None:
                kblk = k_refs[c][0, 0, pl.ds(k0, tq), :]
                sh = jnp.dot(kblk, qp_scr[c], preferred_element_type=F32)
                if diagonal == "first_query_block":
                    sh = sh + jnp.where(qi == 0, bias_scr[...], 0.0)
                elif diagonal:
                    sh = sh + bias_scr[...]
                sh_scr[c] = sh
                mb_scr[c] = jnp.max(sh, axis=0, keepdims=True)
            if value_kb is not None:
                acc_scr[c] = alpha * acc_scr[c] + jnp.dot(vblk, p, preferred_element_type=F32)
                m_scr[c] = m_new

    step(None, 0, diagonal="first_query_block")

    n_plain = jnp.maximum(qi - 1, 0)
    done = 0
    for unroll in KEY_BLOCK_UNROLLS:
        trips = (n_plain - done) // unroll

        def body(i, carry, unroll=unroll, base=done):
            for u in range(unroll):
                step(base + i * unroll + u, base + i * unroll + u + 1)
            return carry

        lax.fori_loop(0, trips, body, 0)
        done = done + trips * unroll

    def finish():
        step(qi, None)
        lam = _lam(lq1_ref, lk1_ref, lq2_ref, lk2_ref, lam_init)
        o1 = acc_scr[0, 0:V_HEAD, :] / acc_scr[0, V_HEAD:V_HEAD + 1, :]
        o2 = acc_scr[1, 0:V_HEAD, :] / acc_scr[1, V_HEAD:V_HEAD + 1, :]
        o = o1 - lam * o2
        o = o * lax.rsqrt(jnp.mean(o * o, axis=0, keepdims=True) + SUBLN_EPS)
        attn = o.T * sw_ref[...] * (1.0 - lam_init)
        o_ref[0] = (jax.nn.silu(ga_ref[0, 0]) * attn).astype(BF16)

    @pl.when(qi > 0)
    def _():
        step(qi - 1, qi, diagonal=True)
        finish()

    @pl.when(qi == 0)
    def _():
        finish()


def _prompt_attention(q, ka, kb, vt, g_attn, lams, subln_w, *, lam_init):
    b, _, s, _ = q.shape
    tq = KEY_BLOCK
    assert s % tq == 0 and tq % CHUNK == 0
    lam_spec = pl.BlockSpec((1, D_HEAD), lambda bi, h, i: (0, 0))
    whole_seq = lambda bi, h, i: (bi, h, 0, 0)
    vmem = 2 * 3 * s * V_HEAD * 2 + (1 + 2) * tq * tq * 4 + 8 * tq * tq * 4 + 12 * tq * ACC_ROWS * 4
    return pl.pallas_call(
        functools.partial(_attn_kernel, lam_init=lam_init),
        grid=(b, N_HEADS, s // tq),
        in_specs=[
            pl.BlockSpec(memory_space=pltpu.SMEM),
            pl.BlockSpec((1, 1, tq, V_HEAD), lambda bi, h, i: (bi, h, i, 0)),
            pl.BlockSpec((1, 1, s, V_HEAD), whole_seq),
            pl.BlockSpec((1, 1, s, V_HEAD), whole_seq),
            pl.BlockSpec((1, 1, V_HEAD, s), whole_seq),
            pl.BlockSpec((1, 1, tq, V_HEAD), lambda bi, h, i: (bi, h, i, 0)),
            lam_spec, lam_spec, lam_spec, lam_spec,
            pl.BlockSpec((1, V_HEAD), lambda bi, h, i: (0, 0)),
        ],
        out_specs=pl.BlockSpec((1, tq, V_HEAD), lambda bi, h, i: (bi, i, h)),
        out_shape=jax.ShapeDtypeStruct((b, s, ATTN_WIDTH), BF16),
        scratch_shapes=[
            pltpu.VMEM((2, V_HEAD, tq), BF16),
            pltpu.VMEM((2, V_HEAD, tq), F32),
            pltpu.VMEM((tq, tq), F32),
            pltpu.VMEM((2, tq, tq), F32),
            pltpu.VMEM((2, 1, tq), F32),
            pltpu.VMEM((2, 1, tq), F32),
            pltpu.VMEM((2, ACC_ROWS, tq), F32),
        ],
        compiler_params=pltpu.CompilerParams(
            dimension_semantics=("arbitrary", "arbitrary", "arbitrary"),
            vmem_limit_bytes=vmem + 8 * MIB,
        ),
        name="attn",
    )(_alibi_sigmas(), q, ka, kb, vt, g_attn, *lams, subln_w.reshape(1, V_HEAD))


def _decode_kernel(sig_ref, q_ref, kn_ref, vn_ref, kc_ref, vc_ref, ga_ref, lq1_ref, lk1_ref, lq2_ref, lk2_ref,
                   sw_ref, o_ref, *, t, past, lam_init):
    hg = pl.program_id(1)
    lam = _lam(lq1_ref, lk1_ref, lq2_ref, lk2_ref, lam_init)
    qpos_c = past + lax.broadcasted_iota(jnp.int32, (t, past), 0)
    kpos_c = lax.broadcasted_iota(jnp.int32, (t, past), 1)
    qpos_n = past + lax.broadcasted_iota(jnp.int32, (t, t), 0)
    kpos_n = past + lax.broadcasted_iota(jnp.int32, (t, t), 1)

    def bias(qpos, kpos, sigma):
        allowed = lax.shift_right_logical(kpos, CHUNK_SHIFT) <= lax.shift_right_logical(qpos, CHUNK_SHIFT)
        return jnp.where(allowed, -sigma * jnp.abs(qpos - kpos).astype(F32), MASK_VALUE)

    lane = lax.broadcasted_iota(jnp.int32, (t, V_HEAD), 1)
    nt = (((1,), (1,)), ((), ()))
    for hh in range(HEADS_PER_TILE):
        sigma = sig_ref[hg * HEADS_PER_TILE + hh]
        cols = slice(hh * V_HEAD, (hh + 1) * V_HEAD)
        q = q_ref[0, hh]
        kc = kc_ref[0, hh].astype(BF16)
        vc = vc_ref[0, hh].astype(BF16)
        kn = kn_ref[0, hh].astype(BF16)
        vn = vn_ref[0, hh].astype(BF16)
        bias_c = bias(qpos_c, kpos_c, sigma)
        bias_n = bias(qpos_n, kpos_n, sigma)
        outs = []
        for c in range(2):
            keep = (lane < D_HEAD) if c == 0 else (lane >= D_HEAD)
            qc = jnp.where(keep, q, jnp.zeros_like(q))
            s_c = lax.dot_general(qc, kc, nt, preferred_element_type=F32) + bias_c
            s_n = lax.dot_general(qc, kn, nt, preferred_element_type=F32) + bias_n
            m = jnp.maximum(jnp.max(s_c, axis=-1, keepdims=True), jnp.max(s_n, axis=-1, keepdims=True))
            p_c = jnp.exp2(s_c - m)
            p_n = jnp.exp2(s_n - m)
            l = jnp.sum(p_c, axis=-1, keepdims=True) + jnp.sum(p_n, axis=-1, keepdims=True)
            pv = (jnp.dot(p_c.astype(BF16), vc, preferred_element_type=F32)
                  + jnp.dot(p_n.astype(BF16), vn, preferred_element_type=F32))
            outs.append(pv / l)
        o = outs[0] - lam * outs[1]
        o = o * lax.rsqrt(jnp.mean(o * o, axis=-1, keepdims=True) + SUBLN_EPS)
        attn = o * sw_ref[...] * (1.0 - lam_init)
        o_ref[0, :, cols] = (jax.nn.silu(ga_ref[0, hh]) * attn).astype(BF16)


def _decode_attention(q, k_new, v_new, cache_k, cache_v, g_attn, lams, subln_w, *, db, lam_init):
    rows = q.shape[2]
    t = rows // db
    past = cache_k.shape[2]
    new_spec = pl.BlockSpec((1, HEADS_PER_TILE, t, V_HEAD), lambda bi, g: (0, g, bi, 0))
    cache_spec = pl.BlockSpec((1, HEADS_PER_TILE, past, V_HEAD), lambda bi, g: (bi, g, 0, 0))
    row_spec = pl.BlockSpec((1, t, COL_TILE), lambda bi, g: (0, bi, g))
    lam_spec = pl.BlockSpec((1, D_HEAD), lambda bi, g: (0, 0))
    vmem = 2 * 2 * past * COL_TILE * 4 + 8 * past * V_HEAD * 4 + 16 * t * past * 4
    return pl.pallas_call(
        functools.partial(_decode_kernel, t=t, past=past, lam_init=lam_init),
        grid=(db, N_HEADS // HEADS_PER_TILE),
        in_specs=[
            pl.BlockSpec(memory_space=pltpu.SMEM),
            new_spec, new_spec, new_spec, cache_spec, cache_spec, new_spec,
            lam_spec, lam_spec, lam_spec, lam_spec,
            pl.BlockSpec((1, V_HEAD), lambda bi, g: (0, 0)),
        ],
        out_specs=row_spec,
        out_shape=jax.ShapeDtypeStruct((1, rows, ATTN_WIDTH), BF16),
        compiler_params=pltpu.CompilerParams(
            dimension_semantics=("arbitrary", "arbitrary"),
            vmem_limit_bytes=vmem + 8 * MIB,
        ),
        name="decode_attn",
    )(_alibi_sigmas(), q, k_new, v_new, cache_k, cache_v, g_attn, *lams, subln_w.reshape(1, V_HEAD))


def _out_kernel(pool_ref, gp_ref, mix_a_ref, x_ref, wo_ref, fw_ref, y_ref):
    mix_p = (jax.nn.silu(gp_ref[0]) * pool_ref[0]).astype(BF16)
    r = jnp.dot(mix_p, wo_ref[0:POOL_WIDTH, :], preferred_element_type=F32)
    r = r + jnp.dot(mix_a_ref[0], wo_ref[POOL_WIDTH:, :], preferred_element_type=F32)
    hres = x_ref[0] + r
    ms = jnp.mean(hres * hres, axis=-1, keepdims=True)
    y_ref[0] = hres * lax.rsqrt(ms + NORM_EPS) * fw_ref[...]


def _merge_out(pool_out, g_pool, mix_attn, x, w_out_bf16, final_norm_w, *, tm):
    b, s, d = x.shape
    assert s % tm == 0
    row = lambda width: pl.BlockSpec((1, tm, width), lambda bi, i: (bi, i, 0))
    vmem = 2 * tm * ((2 * POOL_WIDTH + 2 * d) * 4 + ATTN_WIDTH * 2) + 2 * d * d * 2 + 6 * tm * d * 4
    return pl.pallas_call(
        _out_kernel,
        grid=(b, s // tm),
        in_specs=[
            row(POOL_WIDTH), row(POOL_WIDTH), row(ATTN_WIDTH), row(d),
            pl.BlockSpec((d, d), lambda bi, i: (0, 0)),
            pl.BlockSpec((1, d), lambda bi, i: (0, 0)),
        ],
        out_specs=row(d),
        out_shape=jax.ShapeDtypeStruct((b, s, d), F32),
        compiler_params=pltpu.CompilerParams(
            dimension_semantics=("arbitrary", "arbitrary"),
            vmem_limit_bytes=vmem + 8 * MIB,
        ),
        name="merge_out",
    )(pool_out, g_pool, mix_attn, x, w_out_bf16, final_norm_w.reshape(1, d))


def _tile(n, target):
    t = min(n, target)
    while n % t:
        t //= 2
    return t


def _last_rows(hist, u):
    if u.shape[1] >= POOL_HIST:
        return u[:, u.shape[1] - POOL_HIST:]
    return jnp.concatenate([hist, u], axis=1)[:, -POOL_HIST:]


def kernel(x_prompt, x_sample, cache_k, cache_v, state_pool, norm_w, w_in, w_pool, pool_scale,
           lambda_q1, lambda_k1, lambda_q2, lambda_k2, subln_w, w_out, final_norm_w):
    depth = norm_w.shape[0]
    assert depth == 1, "the final norm is fused into the single layer's output kernel"
    layer = 0
    lam_init = _lam_init(layer)
    b, s, d = x_prompt.shape
    db, t, _ = x_sample.shape
    assert s % KEY_BLOCK == 0

    w_in_b = _retile_w_in(w_in[layer])
    w_out_b = w_out[layer].astype(BF16)
    w_pool_b = w_pool[layer].astype(BF16)
    lams = tuple(p[layer].reshape(1, D_HEAD) for p in (lambda_q1, lambda_k1, lambda_q2, lambda_k2))
    swap_head_axis = lambda a: jnp.transpose(a, (0, 2, 1, 3))

    u, gp, q, k, v, ga, ka, kb, vt = _project(x_prompt, norm_w[layer], w_in_b, tm=_tile(s, 1024), attn_layouts=True)
    hist0 = jnp.zeros((b, HALO, POOL_WIDTH), F32)
    pool_out = _pool_mix(u, hist0, w_pool_b, pool_scale[layer], tm=_tile(s, 512), hist_valid=0)
    mix_attn = _prompt_attention(q, ka, kb, vt, ga, lams, subln_w[layer], lam_init=lam_init)
    y_prompt = _merge_out(pool_out, gp, mix_attn, x_prompt, w_out_b, final_norm_w, tm=_tile(s, 256))

    xs = x_sample.reshape(1, db * t, d)
    u2, gp2, q2, k2, v2, ga2 = _project(xs, norm_w[layer], w_in_b, tm=_tile(db * t, 512), attn_layouts=False)
    hist = jnp.concatenate([jnp.zeros((db, HALO - POOL_HIST, POOL_WIDTH), F32), state_pool[layer]], axis=1)
    u2s = u2.reshape(db, t, POOL_WIDTH)
    pool_out2 = _pool_mix(u2s, hist, w_pool_b, pool_scale[layer], tm=t, hist_valid=POOL_HIST)
    mix_attn2 = _decode_attention(q2, k2, v2, swap_head_axis(cache_k[layer]), swap_head_axis(cache_v[layer]), ga2,
                                  lams, subln_w[layer], db=db, lam_init=lam_init)
    y_sample = _merge_out(pool_out2.reshape(1, db * t, POOL_WIDTH), gp2, mix_attn2, xs, w_out_b, final_norm_w,
                          tm=_tile(db * t, 256)).reshape(db, t, d)

    per_stream = lambda a: swap_head_axis(a).reshape(db, t, N_HEADS, V_HEAD)[None]
    return (y_prompt, y_sample, swap_head_axis(k)[None], swap_head_axis(v)[None], _last_rows(hist0[:, 1:], u)[None],
            per_stream(k2), per_stream(v2), _last_rows(state_pool[layer], u2s)[None])
```

```python
import functools
import math

import jax
import jax.numpy as jnp
import numpy as np
from jax import lax
from jax.experimental import pallas as pl
from jax.experimental.pallas import tpu as pltpu

F32 = jnp.float32
BF16 = jnp.bfloat16

D_MODEL = 2048
CHUNK = 64
CHUNK_SHIFT = 6
POOL_WINDOWS = (2, 4, 8, 16)
POOL_GROUP_DIM = 128
POOL_WIDTH = len(POOL_WINDOWS) * POOL_GROUP_DIM
POOL_HIST = max(POOL_WINDOWS) - 1
HALO = POOL_HIST + 1
ATTN_WIDTH = D_MODEL - POOL_WIDTH
N_HEADS = 12
V_HEAD = ATTN_WIDTH // N_HEADS
D_HEAD = V_HEAD // 2
IN_WIDTH = 2 * POOL_WIDTH + 4 * ATTN_WIDTH
NORM_EPS = 1e-6
SUBLN_EPS = 1e-5
MASK_VALUE = -1e30
LOG2E = math.log2(math.e)
Q_SCALE = (D_HEAD ** -0.5) * LOG2E

COL_TILE = 512
N_COL_TILES = IN_WIDTH // COL_TILE
ATTN_COL_TILES = ATTN_WIDTH // COL_TILE
HEADS_PER_TILE = COL_TILE // V_HEAD
SEG_GP, SEG_Q, SEG_K, SEG_V, SEG_GA = 1, 2, 2 + ATTN_COL_TILES, 2 + 2 * ATTN_COL_TILES, 2 + 3 * ATTN_COL_TILES

KEY_BLOCK = 512
N_POS_FEATURES = 9
BF16_SUBLANES = 16
MIB = 1024 * 1024


def _alibi_sigmas():
    def pow2_slopes(m):
        start = 2.0 ** (-8.0 / m)
        return [start ** (i + 1) for i in range(m)]

    n = N_HEADS
    if math.log2(n).is_integer():
        s = pow2_slopes(n)
    else:
        c = 2 ** int(math.floor(math.log2(n)))
        s = pow2_slopes(c) + pow2_slopes(2 * c)[0::2][: n - c]
    return jnp.asarray(np.array(s, dtype=np.float32) * np.float32(LOG2E))


def _lam_init(layer):
    return 0.8 - 0.6 * math.exp(-0.3 * layer)


def _lam(lq1_ref, lk1_ref, lq2_ref, lk2_ref, lam_init):
    a = jnp.sum(lq1_ref[...] * lk1_ref[...], axis=-1, keepdims=True)
    b = jnp.sum(lq2_ref[...] * lk2_ref[...], axis=-1, keepdims=True)
    return jnp.exp(a) - jnp.exp(b) + lam_init


def _bf16_parts(x):
    p1 = x.astype(BF16).astype(F32)
    p2 = (x - p1).astype(BF16).astype(F32)
    p3 = (x - p1 - p2).astype(BF16).astype(F32)
    return (p1, p2, p3)


def _position_features(kj, slot):
    hi = lax.shift_left(lax.shift_right_logical(kj, 4), 4).astype(F32)
    lo = (kj & 15).astype(F32)
    return jnp.where(slot < 0, 0.0, jnp.where(slot < 3, hi, jnp.where(slot < 6, lo, jnp.where(slot < 9, 1.0, 0.0))))


def _retile_kernel(w_ref, o_ref):
    o_ref[0] = w_ref[...].astype(BF16)


def _retile_w_in(w_in):
    d = w_in.shape[0]
    return pl.pallas_call(
        _retile_kernel,
        grid=(N_COL_TILES,),
        in_specs=[pl.BlockSpec((d, COL_TILE), lambda j: (0, j))],
        out_specs=pl.BlockSpec((1, d, COL_TILE), lambda j: (j, 0, 0)),
        out_shape=jax.ShapeDtypeStruct((N_COL_TILES, d, COL_TILE), BF16),
        compiler_params=pltpu.CompilerParams(dimension_semantics=("arbitrary",)),
        name="retile_w_in",
    )(w_in)


def _proj_kernel(x_ref, nw_ref, w_ref, u_ref, gp_ref, q_ref, k_ref, v_ref, ga_ref, *rest, tm, attn_layouts):
    if attn_layouts:
        ka_ref, kb_ref, vt_ref, h_scr = rest
    else:
        (h_scr,) = rest
    i = pl.program_id(1)
    j = pl.program_id(2)

    @pl.when(j == 0)
    def _():
        x = x_ref[0]
        ms = jnp.mean(x * x, axis=-1, keepdims=True)
        h_scr[...] = (x * lax.rsqrt(ms + NORM_EPS) * nw_ref[...]).astype(BF16)

    def tile():
        return jnp.dot(h_scr[...], w_ref[0], preferred_element_type=F32)

    @pl.when(j == 0)
    def _():
        u_ref[0] = tile()

    @pl.when(j == SEG_GP)
    def _():
        gp_ref[0] = tile()

    @pl.when((j >= SEG_Q) & (j < SEG_K))
    def _():
        z = tile()
        for hh in range(HEADS_PER_TILE):
            q_ref[0, hh] = (z[:, hh * V_HEAD:(hh + 1) * V_HEAD] * Q_SCALE).astype(BF16)

    @pl.when((j >= SEG_K) & (j < SEG_V))
    def _():
        z = tile()
        if attn_layouts:
            lane = lax.broadcasted_iota(jnp.int32, (tm, V_HEAD), 1)
            kj = (i * tm + lax.broadcasted_iota(jnp.int32, (tm, V_HEAD), 0)) & (KEY_BLOCK - 1)
            feat_a = _position_features(kj, lane - D_HEAD)
            feat_b = _position_features(kj, lane)
        for hh in range(HEADS_PER_TILE):
            zh = z[:, hh * V_HEAD:(hh + 1) * V_HEAD]
            k_ref[0, hh] = zh
            if attn_layouts:
                ka_ref[0, hh] = jnp.where(lane < D_HEAD, zh, feat_a).astype(BF16)
                kb_ref[0, hh] = jnp.where(lane >= D_HEAD, zh, feat_b).astype(BF16)

    @pl.when((j >= SEG_V) & (j < SEG_GA))
    def _():
        z = tile()
        for hh in range(HEADS_PER_TILE):
            zh = z[:, hh * V_HEAD:(hh + 1) * V_HEAD]
            v_ref[0, hh] = zh
            if attn_layouts:
                vt_ref[0, hh] = zh.T.astype(BF16)

    @pl.when(j >= SEG_GA)
    def _():
        z = tile()
        for hh in range(HEADS_PER_TILE):
            ga_ref[0, hh] = z[:, hh * V_HEAD:(hh + 1) * V_HEAD]


def _project(x, norm_w, w_in_bf16, *, tm, attn_layouts):
    b, s, d = x.shape
    assert s % tm == 0 and d == D_MODEL
    grid = (b, s // tm, N_COL_TILES)

    def seg(first, j):
        return jnp.clip(j - first, 0, ATTN_COL_TILES - 1)

    row_tile = lambda bi, i, j: (bi, i, 0)
    head_block = (1, HEADS_PER_TILE, tm, V_HEAD)
    head_map = lambda first: (lambda bi, i, j: (bi, seg(first, j), i, 0))
    out_shape = [
        jax.ShapeDtypeStruct((b, s, POOL_WIDTH), F32),
        jax.ShapeDtypeStruct((b, s, POOL_WIDTH), F32),
        jax.ShapeDtypeStruct((b, N_HEADS, s, V_HEAD), BF16),
        jax.ShapeDtypeStruct((b, N_HEADS, s, V_HEAD), F32),
        jax.ShapeDtypeStruct((b, N_HEADS, s, V_HEAD), F32),
        jax.ShapeDtypeStruct((b, N_HEADS, s, V_HEAD), F32),
    ]
    out_specs = [
        pl.BlockSpec((1, tm, COL_TILE), row_tile),
        pl.BlockSpec((1, tm, COL_TILE), row_tile),
        pl.BlockSpec(head_block, head_map(SEG_Q)),
        pl.BlockSpec(head_block, head_map(SEG_K)),
        pl.BlockSpec(head_block, head_map(SEG_V)),
        pl.BlockSpec(head_block, head_map(SEG_GA)),
    ]
    n_f32_tiles, n_bf16_tiles = 5, 1
    if attn_layouts:
        out_shape += [
            jax.ShapeDtypeStruct((b, N_HEADS, s, V_HEAD), BF16),
            jax.ShapeDtypeStruct((b, N_HEADS, s, V_HEAD), BF16),
            jax.ShapeDtypeStruct((b, N_HEADS, V_HEAD, s), BF16),
        ]
        out_specs += [
            pl.BlockSpec(head_block, head_map(SEG_K)),
            pl.BlockSpec(head_block, head_map(SEG_K)),
            pl.BlockSpec((1, HEADS_PER_TILE, V_HEAD, tm), lambda bi, i, j: (bi, seg(SEG_V, j), 0, i)),
        ]
        n_bf16_tiles += 3
    vmem = (2 * tm * d * 4 + tm * d * 2 + 2 * d * COL_TILE * 2
            + 2 * tm * COL_TILE * (n_f32_tiles * 4 + n_bf16_tiles * 2) + 3 * tm * COL_TILE * 4)
    return pl.pallas_call(
        functools.partial(_proj_kernel, tm=tm, attn_layouts=attn_layouts),
        grid=grid,
        in_specs=[
            pl.BlockSpec((1, tm, d), row_tile),
            pl.BlockSpec((1, d), lambda bi, i, j: (0, 0)),
            pl.BlockSpec((1, d, COL_TILE), lambda bi, i, j: (j, 0, 0)),
        ],
        out_specs=out_specs,
        out_shape=out_shape,
        scratch_shapes=[pltpu.VMEM((tm, d), BF16)],
        compiler_params=pltpu.CompilerParams(
            dimension_semantics=("arbitrary", "arbitrary", "arbitrary"),
            vmem_limit_bytes=vmem + 4 * MIB,
        ),
        name="proj",
    )(x, norm_w.reshape(1, d), w_in_bf16)


def _pooled_groups(u_ref, prev_ref, hist_ref, wp_ref, ps_ref, ext_scr, *, tm, hist_valid):
    i = pl.program_id(1)
    ext_scr[0:HALO, :] = jnp.where(i == 0, hist_ref[0], prev_ref[0])
    ext_scr[HALO:HALO + tm, :] = u_ref[0]
    frame = i * tm + lax.broadcasted_iota(jnp.int32, (tm, POOL_GROUP_DIM), 0)
    out = []
    for g, w in enumerate(POOL_WINDOWS):
        lanes = slice(g * POOL_GROUP_DIM, (g + 1) * POOL_GROUP_DIM)
        cur = ext_scr[HALO:HALO + tm, lanes]
        acc = cur
        for dlt in range(1, w):
            acc = acc + ext_scr[HALO - dlt:HALO - dlt + tm, lanes]
        count = jnp.minimum(w, hist_valid + frame + 1).astype(F32)
        diff = acc / count - cur
        y = jnp.dot(diff.astype(BF16), wp_ref[g], preferred_element_type=F32)
        out.append(y * ps_ref[:, lanes])
    return out


def _pool_in_specs(tm, c):
    blocks_per_tile = tm // HALO
    return [
        pl.BlockSpec((1, tm, c), lambda bi, i: (bi, i, 0)),
        pl.BlockSpec((1, HALO, c), lambda bi, i: (bi, jnp.maximum(i * blocks_per_tile - 1, 0), 0)),
        pl.BlockSpec((1, HALO, c), lambda bi, i: (bi, 0, 0)),
        pl.BlockSpec((len(POOL_WINDOWS), POOL_GROUP_DIM, POOL_GROUP_DIM), lambda bi, i: (0, 0, 0)),
        pl.BlockSpec((1, c), lambda bi, i: (0, 0)),
    ]


def _pool_kernel(u_ref, prev_ref, hist_ref, wp_ref, ps_ref, o_ref, ext_scr, *, tm, hist_valid):
    groups = _pooled_groups(u_ref, prev_ref, hist_ref, wp_ref, ps_ref, ext_scr, tm=tm, hist_valid=hist_valid)
    for g, y in enumerate(groups):
        o_ref[0, :, g * POOL_GROUP_DIM:(g + 1) * POOL_GROUP_DIM] = y


def _pool_mix(u, hist, w_pool_bf16, pool_scale, *, tm, hist_valid):
    b, t, c = u.shape
    assert t % tm == 0 and tm % HALO == 0
    return pl.pallas_call(
        functools.partial(_pool_kernel, tm=tm, hist_valid=hist_valid),
        grid=(b, t // tm),
        in_specs=_pool_in_specs(tm, c),
        out_specs=pl.BlockSpec((1, tm, c), lambda bi, i: (bi, i, 0)),
        out_shape=jax.ShapeDtypeStruct((b, t, c), F32),
        scratch_shapes=[pltpu.VMEM((HALO + tm, c), F32)],
        compiler_params=pltpu.CompilerParams(dimension_semantics=("arbitrary", "arbitrary")),
        name="pool",
    )(u, u, hist, w_pool_bf16, pool_scale.reshape(1, c))


ACC_ROWS = V_HEAD + BF16_SUBLANES
KEY_BLOCK_UNROLLS = (4, 2, 1)


def _attn_kernel(sig_ref, q_ref, ka_ref, kb_ref, vt_ref, ga_ref, lq1_ref, lk1_ref, lq2_ref, lk2_ref, sw_ref, o_ref,
                 qp_scr, qfeat_scr, bias_scr, sh_scr, mb_scr, m_scr, acc_scr, *, lam_init):
    tq = KEY_BLOCK
    h = pl.program_id(1)
    qi = pl.program_id(2)
    sigma = sig_ref[h]
    decay = sigma * tq
    k_refs = (ka_ref, kb_ref)

    @pl.when(qi == 0)
    def _():
        kj = lax.broadcasted_iota(jnp.int32, (tq, tq), 0)
        qq = lax.broadcasted_iota(jnp.int32, (tq, tq), 1)
        rel = (qq - kj).astype(F32)
        allowed = lax.shift_right_logical(kj, CHUNK_SHIFT) <= lax.shift_right_logical(qq, CHUNK_SHIFT)
        bias_scr[...] = jnp.where(allowed, jnp.where(rel >= 0.0, 0.0, 2.0 * sigma * rel), MASK_VALUE)
        row = lax.broadcasted_iota(jnp.int32, (V_HEAD, tq), 0)
        col = lax.broadcasted_iota(jnp.int32, (V_HEAD, tq), 1).astype(F32)
        parts = _bf16_parts(jnp.full((V_HEAD, tq), sigma, F32)) * 2 + _bf16_parts(-sigma * col)
        assert len(parts) == N_POS_FEATURES
        for c, slot in enumerate((row - D_HEAD, row)):
            feat = jnp.zeros((V_HEAD, tq), F32)
            for n, part in enumerate(parts):
                feat = jnp.where(slot == n, part, feat)
            qfeat_scr[c] = feat

    qt = q_ref[0, 0].astype(F32).T
    row = lax.broadcasted_iota(jnp.int32, (V_HEAD, tq), 0)
    qp_scr[0] = jnp.where(row < D_HEAD, qt, qfeat_scr[0]).astype(BF16)
    qp_scr[1] = jnp.where(row >= D_HEAD, qt, qfeat_scr[1]).astype(BF16)
    m_scr[...] = jnp.full(m_scr.shape, MASK_VALUE, F32)
    acc_scr[...] = jnp.zeros(acc_scr.shape, F32)
    ones_rows = jnp.ones((BF16_SUBLANES, tq), BF16)

    def step(value_kb, score_kb, diagonal=False):
        if score_kb is not None:
            k0 = pl.multiple_of(score_kb * tq, tq)
        if value_kb is not None:
            v0 = pl.multiple_of(value_kb * tq, tq)
            vblk = jnp.concatenate([vt_ref[0, 0, :, pl.ds(v0, tq)], ones_rows], axis=0)
        for c in range(2):
            if value_kb is not None:
                m_old = m_scr[c] - decay
                m_new = jnp.maximum(m_old, mb_scr[c])
                p = jnp.exp2(sh_scr[c] - m_new).astype(BF16)
                alpha = jnp.exp2(m_old - m_new)
            if score_kb is not None:
                kblk = k_refs[c][0, 0, pl.ds(k0, tq), :]
                sh = jnp.dot(kblk, qp_scr[c], preferred_element_type=F32)
                if diagonal == "first_query_block":
                    sh = sh + jnp.where(qi == 0, bias_scr[...], 0.0)
                elif diagonal:
                    sh = sh + bias_scr[...]
                sh_scr[c] = sh
                mb_scr[c] = jnp.max(sh, axis=0, keepdims=True)
            if value_kb is not None:
                acc_scr[c] = alpha * acc_scr[c] + jnp.dot(vblk, p, preferred_element_type=F32)
                m_scr[c] = m_new

    step(None, 0, diagonal="first_query_block")

    n_plain = jnp.maximum(qi - 1, 0)
    done = 0
    for unroll in KEY_BLOCK_UNROLLS:
        trips = (n_plain - done) // unroll

        def body(i, carry, unroll=unroll, base=done):
            for u in range(unroll):
                step(base + i * unroll + u, base + i * unroll + u + 1)
            return carry

        lax.fori_loop(0, trips, body, 0)
        done = done + trips * unroll

    def finish():
        step(qi, None)
        lam = _lam(lq1_ref, lk1_ref, lq2_ref, lk2_ref, lam_init)
        o1 = acc_scr[0, 0:V_HEAD, :] / acc_scr[0, V_HEAD:V_HEAD + 1, :]
        o2 = acc_scr[1, 0:V_HEAD, :] / acc_scr[1, V_HEAD:V_HEAD + 1, :]
        o = o1 - lam * o2
        o = o * lax.rsqrt(jnp.mean(o * o, axis=0, keepdims=True) + SUBLN_EPS)
        attn = o.T * sw_ref[...] * (1.0 - lam_init)
        o_ref[0] = (jax.nn.silu(ga_ref[0, 0]) * attn).astype(BF16)

    @pl.when(qi > 0)
    def _():
        step(qi - 1, qi, diagonal=True)
        finish()

    @pl.when(qi == 0)
    def _():
        finish()


def _prompt_attention(q, ka, kb, vt, g_attn, lams, subln_w, *, lam_init):
    b, _, s, _ = q.shape
    tq = KEY_BLOCK
    assert s % tq == 0 and tq % CHUNK == 0
    lam_spec = pl.BlockSpec((1, D_HEAD), lambda bi, h, i: (0, 0))
    whole_seq = lambda bi, h, i: (bi, h, 0, 0)
    vmem = 2 * 3 * s * V_HEAD * 2 + (1 + 2) * tq * tq * 4 + 8 * tq * tq * 4 + 12 * tq * ACC_ROWS * 4
    return pl.pallas_call(
        functools.partial(_attn_kernel, lam_init=lam_init),
        grid=(b, N_HEADS, s // tq),
        in_specs=[
            pl.BlockSpec(memory_space=pltpu.SMEM),
            pl.BlockSpec((1, 1, tq, V_HEAD), lambda bi, h, i: (bi, h, i, 0)),
            pl.BlockSpec((1, 1, s, V_HEAD), whole_seq),
            pl.BlockSpec((1, 1, s, V_HEAD), whole_seq),
            pl.BlockSpec((1, 1, V_HEAD, s), whole_seq),
            pl.BlockSpec((1, 1, tq, V_HEAD), lambda bi, h, i: (bi, h, i, 0)),
            lam_spec, lam_spec, lam_spec, lam_spec,
            pl.BlockSpec((1, V_HEAD), lambda bi, h, i: (0, 0)),
        ],
        out_specs=pl.BlockSpec((1, tq, V_HEAD), lambda bi, h, i: (bi, i, h)),
        out_shape=jax.ShapeDtypeStruct((b, s, ATTN_WIDTH), BF16),
        scratch_shapes=[
            pltpu.VMEM((2, V_HEAD, tq), BF16),
            pltpu.VMEM((2, V_HEAD, tq), F32),
            pltpu.VMEM((tq, tq), F32),
            pltpu.VMEM((2, tq, tq), F32),
            pltpu.VMEM((2, 1, tq), F32),
            pltpu.VMEM((2, 1, tq), F32),
            pltpu.VMEM((2, ACC_ROWS, tq), F32),
        ],
        compiler_params=pltpu.CompilerParams(
            dimension_semantics=("arbitrary", "arbitrary", "arbitrary"),
            vmem_limit_bytes=vmem + 8 * MIB,
        ),
        name="attn",
    )(_alibi_sigmas(), q, ka, kb, vt, g_attn, *lams, subln_w.reshape(1, V_HEAD))


def _decode_kernel(sig_ref, q_ref, kn_ref, vn_ref, kc_ref, vc_ref, ga_ref, lq1_ref, lk1_ref, lq2_ref, lk2_ref,
                   sw_ref, o_ref, *, t, past, lam_init):
    hg = pl.program_id(1)
    lam = _lam(lq1_ref, lk1_ref, lq2_ref, lk2_ref, lam_init)
    qpos_c = past + lax.broadcasted_iota(jnp.int32, (t, past), 0)
    kpos_c = lax.broadcasted_iota(jnp.int32, (t, past), 1)
    qpos_n = past + lax.broadcasted_iota(jnp.int32, (t, t), 0)
    kpos_n = past + lax.broadcasted_iota(jnp.int32, (t, t), 1)

    def bias(qpos, kpos, sigma):
        allowed = lax.shift_right_logical(kpos, CHUNK_SHIFT) <= lax.shift_right_logical(qpos, CHUNK_SHIFT)
        return jnp.where(allowed, -sigma * jnp.abs(qpos - kpos).astype(F32), MASK_VALUE)

    lane = lax.broadcasted_iota(jnp.int32, (t, V_HEAD), 1)
    nt = (((1,), (1,)), ((), ()))
    for hh in range(HEADS_PER_TILE):
        sigma = sig_ref[hg * HEADS_PER_TILE + hh]
        cols = slice(hh * V_HEAD, (hh + 1) * V_HEAD)
        q = q_ref[0, hh]
        kc = kc_ref[0, hh].astype(BF16)
        vc = vc_ref[0, hh].astype(BF16)
        kn = kn_ref[0, hh].astype(BF16)
        vn = vn_ref[0, hh].astype(BF16)
        bias_c = bias(qpos_c, kpos_c, sigma)
        bias_n = bias(qpos_n, kpos_n, sigma)
        outs = []
        for c in range(2):
            keep = (lane < D_HEAD) if c == 0 else (lane >= D_HEAD)
            qc = jnp.where(keep, q, jnp.zeros_like(q))
            s_c = lax.dot_general(qc, kc, nt, preferred_element_type=F32) + bias_c
            s_n = lax.dot_general(qc, kn, nt, preferred_element_type=F32) + bias_n
            m = jnp.maximum(jnp.max(s_c, axis=-1, keepdims=True), jnp.max(s_n, axis=-1, keepdims=True))
            p_c = jnp.exp2(s_c - m)
            p_n = jnp.exp2(s_n - m)
            l = jnp.sum(p_c, axis=-1, keepdims=True) + jnp.sum(p_n, axis=-1, keepdims=True)
            pv = (jnp.dot(p_c.astype(BF16), vc, preferred_element_type=F32)
                  + jnp.dot(p_n.astype(BF16), vn, preferred_element_type=F32))
            outs.append(pv / l)
        o = outs[0] - lam * outs[1]
        o = o * lax.rsqrt(jnp.mean(o * o, axis=-1, keepdims=True) + SUBLN_EPS)
        attn = o * sw_ref[...] * (1.0 - lam_init)
        o_ref[0, :, cols] = (jax.nn.silu(ga_ref[0, hh]) * attn).astype(BF16)


def _decode_attention(q, k_new, v_new, cache_k, cache_v, g_attn, lams, subln_w, *, db, lam_init):
    rows = q.shape[2]
    t = rows // db
    past = cache_k.shape[2]
    new_spec = pl.BlockSpec((1, HEADS_PER_TILE, t, V_HEAD), lambda bi, g: (0, g, bi, 0))
    cache_spec = pl.BlockSpec((1, HEADS_PER_TILE, past, V_HEAD), lambda bi, g: (bi, g, 0, 0))
    row_spec = pl.BlockSpec((1, t, COL_TILE), lambda bi, g: (0, bi, g))
    lam_spec = pl.BlockSpec((1, D_HEAD), lambda bi, g: (0, 0))
    vmem = 2 * 2 * past * COL_TILE * 4 + 8 * past * V_HEAD * 4 + 16 * t * past * 4
    return pl.pallas_call(
        functools.partial(_decode_kernel, t=t, past=past, lam_init=lam_init),
        grid=(db, N_HEADS // HEADS_PER_TILE),
        in_specs=[
            pl.BlockSpec(memory_space=pltpu.SMEM),
            new_spec, new_spec, new_spec, cache_spec, cache_spec, new_spec,
            lam_spec, lam_spec, lam_spec, lam_spec,
            pl.BlockSpec((1, V_HEAD), lambda bi, g: (0, 0)),
        ],
        out_specs=row_spec,
        out_shape=jax.ShapeDtypeStruct((1, rows, ATTN_WIDTH), BF16),
        compiler_params=pltpu.CompilerParams(
            dimension_semantics=("arbitrary", "arbitrary"),
            vmem_limit_bytes=vmem + 8 * MIB,
        ),
        name="decode_attn",
    )(_alibi_sigmas(), q, k_new, v_new, cache_k, cache_v, g_attn, *lams, subln_w.reshape(1, V_HEAD))


def _out_kernel(pool_ref, gp_ref, mix_a_ref, x_ref, wo_ref, fw_ref, y_ref):
    mix_p = (jax.nn.silu(gp_ref[0]) * pool_ref[0]).astype(BF16)
    r = jnp.dot(mix_p, wo_ref[0:POOL_WIDTH, :], preferred_element_type=F32)
    r = r + jnp.dot(mix_a_ref[0], wo_ref[POOL_WIDTH:, :], preferred_element_type=F32)
    hres = x_ref[0] + r
    ms = jnp.mean(hres * hres, axis=-1, keepdims=True)
    y_ref[0] = hres * lax.rsqrt(ms + NORM_EPS) * fw_ref[...]


def _merge_out(pool_out, g_pool, mix_attn, x, w_out_bf16, final_norm_w, *, tm):
    b, s, d = x.shape
    assert s % tm == 0
    row = lambda width: pl.BlockSpec((1, tm, width), lambda bi, i: (bi, i, 0))
    vmem = 2 * tm * ((2 * POOL_WIDTH + 2 * d) * 4 + ATTN_WIDTH * 2) + 2 * d * d * 2 + 6 * tm * d * 4
    return pl.pallas_call(
        _out_kernel,
        grid=(b, s // tm),
        in_specs=[
            row(POOL_WIDTH), row(POOL_WIDTH), row(ATTN_WIDTH), row(d),
            pl.BlockSpec((d, d), lambda bi, i: (0, 0)),
            pl.BlockSpec((1, d), lambda bi, i: (0, 0)),
        ],
        out_specs=row(d),
        out_shape=jax.ShapeDtypeStruct((b, s, d), F32),
        compiler_params=pltpu.CompilerParams(
            dimension_semantics=("arbitrary", "arbitrary"),
            vmem_limit_bytes=vmem + 8 * MIB,
        ),
        name="merge_out",
    )(pool_out, g_pool, mix_attn, x, w_out_bf16, final_norm_w.reshape(1, d))


def _tile(n, target):
    t = min(n, target)
    while n % t:
        t //= 2
    return t


def _last_rows(hist, u):
    if u.shape[1] >= POOL_HIST:
        return u[:, u.shape[1] - POOL_HIST:]
    return jnp.concatenate([hist, u], axis=1)[:, -POOL_HIST:]


def kernel(x_prompt, x_sample, cache_k, cache_v, state_pool, norm_w, w_in, w_pool, pool_scale,
           lambda_q1, lambda_k1, lambda_q2, lambda_k2, subln_w, w_out, final_norm_w):
    depth = norm_w.shape[0]
    assert depth == 1, "the final norm is fused into the single layer's output kernel"
    layer = 0
    lam_init = _lam_init(layer)
    b, s, d = x_prompt.shape
    db, t, _ = x_sample.shape
    assert s % KEY_BLOCK == 0

    w_in_b = _retile_w_in(w_in[layer])
    w_out_b = w_out[layer].astype(BF16)
    w_pool_b = w_pool[layer].astype(BF16)
    lams = tuple(p[layer].reshape(1, D_HEAD) for p in (lambda_q1, lambda_k1, lambda_q2, lambda_k2))
    swap_head_axis = lambda a: jnp.transpose(a, (0, 2, 1, 3))

    u, gp, q, k, v, ga, ka, kb, vt = _project(x_prompt, norm_w[layer], w_in_b, tm=_tile(s, 1024), attn_layouts=True)
    hist0 = jnp.zeros((b, HALO, POOL_WIDTH), F32)
    pool_out = _pool_mix(u, hist0, w_pool_b, pool_scale[layer], tm=_tile(s, 512), hist_valid=0)
    mix_attn = _prompt_attention(q, ka, kb, vt, ga, lams, subln_w[layer], lam_init=lam_init)
    y_prompt = _merge_out(pool_out, gp, mix_attn, x_prompt, w_out_b, final_norm_w, tm=_tile(s, 256))

    xs = x_sample.reshape(1, db * t, d)
    u2, gp2, q2, k2, v2, ga2 = _project(xs, norm_w[layer], w_in_b, tm=_tile(db * t, 512), attn_layouts=False)
    hist = jnp.concatenate([jnp.zeros((db, HALO - POOL_HIST, POOL_WIDTH), F32), state_pool[layer]], axis=1)
    u2s = u2.reshape(db, t, POOL_WIDTH)
    pool_out2 = _pool_mix(u2s, hist, w_pool_b, pool_scale[layer], tm=t, hist_valid=POOL_HIST)
    mix_attn2 = _decode_attention(q2, k2, v2, swap_head_axis(cache_k[layer]), swap_head_axis(cache_v[layer]), ga2,
                                  lams, subln_w[layer], db=db, lam_init=lam_init)
    y_sample = _merge_out(pool_out2.reshape(1, db * t, POOL_WIDTH), gp2, mix_attn2, xs, w_out_b, final_norm_w,
                          tm=_tile(db * t, 256)).reshape(db, t, d)

    per_stream = lambda a: swap_head_axis(a).reshape(db, t, N_HEADS, V_HEAD)[None]
    return (y_prompt, y_sample, swap_head_axis(k)[None], swap_head_axis(v)[None], _last_rows(hist0[:, 1:], u)[None],
            per_stream(k2), per_stream(v2), _last_rows(state_pool[layer], u2s)[None])
```
